```python
import jax, jax.numpy as jnp
from jax import lax
import numpy as np

D_MODEL = 2048
BATCH = 32
SEQ = 256
DEPTH = 4
DEC_BATCH = 4
DEC_SEQ = 2048
PAST_LEN = 256

GRID_W = 64
HEAD_DIM = 128
A_W = 3 * D_MODEL // 8
H_A = A_W // HEAD_DIM
B_W = D_MODEL // 4
C_W = D_MODEL - A_W - B_W
H_C = C_W // HEAD_DIM
MIX_W = A_W + B_W + C_W
IN_COLS = 5 * A_W + 2 * B_W + 3 * C_W
CHUNK = 16
CONV_W = 31
KH_MAX = 8
KW = 16
QW = 16
KB = QW + KW
NCB = GRID_W // QW
ROW_BLOCK = 2
Q_BLOCK = 128
N_EXPERTS = 32
TOP_K = 4
D_FF = D_MODEL
SWIGLU_ALPHA = 1.702
SWIGLU_LIMIT = 7.0
MOE_BLOCK = 128
EPS = 1e-6
NEG_BIG = -1e30
LB_FLOOR = 1e-30
F32 = jnp.float32

kernel_name = 'hybrid_hgrn2_conv_natten_moe_diffusion_step'


def _rmsnorm(x, g):
    xf = x.astype(F32)
    y = xf * lax.rsqrt(jnp.mean(xf * xf, axis=-1, keepdims=True) + EPS)
    return (y * g.astype(F32)).astype(x.dtype)


def _modulation(cond, w_mod_l, b_mod_l):
    m = jax.nn.silu(cond) @ w_mod_l + b_mod_l
    return jnp.split(m, 6, axis=-1)


def _project(h, w_in):
    z = h @ w_in
    sizes = [A_W] * 5 + [2 * B_W] + [C_W] * 3
    idx = np.cumsum(sizes)[:-1].tolist()
    return jnp.split(z, idx, axis=-1)


def _gla_chunked(q, k, v, log_f, s0):
    B, T, H, DK = q.shape
    DV = v.shape[-1]
    N = T // CHUNK

    def blocks(a):
        return a.astype(F32).reshape(B, N, CHUNK, H, a.shape[-1]).transpose(1, 0, 3, 2, 4)

    qc, kc, vc, gc = blocks(q), blocks(k), blocks(v), blocks(log_f)
    b = jnp.cumsum(gc, axis=3)
    b_last = b[:, :, :, -1, :]
    lower = np.tril(np.ones((CHUNK, CHUNK), dtype=bool))
    rel = jnp.where(lower[:, :, None], b[..., :, None, :] - b[..., None, :, :], NEG_BIG)
    scores = jnp.einsum('nbhtd,nbhsd,nbhtsd->nbhts', qc, kc, jnp.exp(rel))
    o_intra = jnp.einsum('nbhts,nbhse->nbhte', scores, vc)
    q_dec = qc * jnp.exp(b)
    k_dec = kc * jnp.exp(b_last[:, :, :, None, :] - b)

    def step(s, xs):
        qd, kd, vv, bl = xs
        o = jnp.einsum('bhtd,bhde->bhte', qd, s)
        s = s * jnp.exp(bl)[..., None] + jnp.einsum('bhtd,bhte->bhde', kd, vv)
        return s, o

    s_fin, o_inter = lax.scan(step, s0.astype(F32), (q_dec, k_dec, vc, b_last))
    o = (o_intra + o_inter).transpose(1, 0, 3, 2, 4).reshape(B, T, H, DV)
    return o, s_fin


def _hgrn2(q_raw, v_raw, zf_f, zf_b, og, lb_f, lb_b, gn_g, s0_f, s0_b):
    B, T, _ = q_raw.shape

    def heads(a):
        return a.reshape(B, T, H_A, HEAD_DIM)

    q = heads(jax.nn.silu(q_raw.astype(F32))) * HEAD_DIM ** -0.5
    v = heads(v_raw.astype(F32))

    def gates(z, lb):
        z = z.astype(F32)
        log_f = jnp.logaddexp(jnp.log(jnp.maximum(lb, LB_FLOOR)),
                              jnp.log1p(-lb) + jax.nn.log_sigmoid(z))
        k = (1.0 - lb) * jax.nn.sigmoid(-z)
        return heads(log_f), heads(k)

    lf_f, k_f = gates(zf_f, lb_f)
    lf_b, k_b = gates(zf_b, lb_b)
    o_f, s_f = _gla_chunked(q, k_f, v, lf_f, s0_f)

    def rev(a):
        return jnp.flip(a, axis=1)

    o_b, s_b = _gla_chunked(rev(q), rev(k_b), rev(v), rev(lf_b), s0_b)
    o = o_f + rev(o_b)
    o = o * lax.rsqrt(jnp.mean(o * o, axis=-1, keepdims=True) + EPS) * gn_g.astype(F32)
    o = o * jax.nn.silu(heads(og.astype(F32)))
    return o.reshape(B, T, A_W).astype(q_raw.dtype), s_f, s_b


def _conv_module(glu_in, lw):
    a, g = jnp.split(glu_in, 2, axis=-1)
    u = a * jax.nn.sigmoid(g)
    u = lax.conv_general_dilated(u, lw['dw_w'][:, None, :], (1,), [(CONV_W // 2, CONV_W // 2)],
                                 dimension_numbers=('NWC', 'WIO', 'NWC'),
                                 feature_group_count=B_W) + lw['dw_b']
    uf = u.astype(F32)
    mu = jnp.mean(uf, axis=-1, keepdims=True)
    var = jnp.mean(jnp.square(uf - mu), axis=-1, keepdims=True)
    uf = (uf - mu) * lax.rsqrt(var + EPS) * lw['ln_g'].astype(F32) + lw['ln_b'].astype(F32)
    u = jax.nn.silu(uf).astype(glu_in.dtype)
    return u @ lw['pw_w'] + lw['pw_b']


def _context_attention(q, k, v):
    B, S, H, Dh = q.shape
    nq = S // Q_BLOCK
    qb = q.reshape(B, nq, Q_BLOCK, H, Dh).transpose(1, 0, 2, 3, 4)
    scale = Dh ** -0.5

    def blk(qi):
        s = jnp.einsum('bqhd,bkhd->bhqk', qi, k).astype(F32) * scale
        p = jax.nn.softmax(s, axis=-1).astype(v.dtype)
        return jnp.einsum('bhqk,bkhd->bqhd', p, v)

    o = lax.map(blk, qb)
    return o.transpose(1, 0, 2, 3, 4).reshape(B, S, H, Dh)


def _na_latent(q, k, v, k_ctx, v_ctx, rpb):
    B, T, H, Dh = q.shape
    rows = T // GRID_W
    kh = min(KH_MAX, rows)
    nblk = rows // ROW_BLOCK
    r = np.arange(rows)
    row_idx = np.clip(r - kh // 2, 0, rows - kh)[:, None] + np.arange(kh)[None, :]
    qcol = np.arange(GRID_W).reshape(NCB, QW)
    col_idx = np.clip(qcol[:, 0] - KW // 2, 0, GRID_W - KB)[:, None] + np.arange(KB)[None, :]
    win_start = np.clip(qcol - KW // 2, 0, GRID_W - KW)
    in_win = (col_idx[:, None, :] >= win_start[..., None]) & (col_idx[:, None, :] < win_start[..., None] + KW)
    dr = row_idx - r[:, None] + KH_MAX - 1
    dc = np.clip(col_idx[:, None, :] - qcol[..., None], -(KW - 1), KW - 1) + KW - 1
    bias = rpb.astype(F32)[:, dr[:, None, None, :, None], dc[None, :, :, None, :]]
    bias = jnp.where(in_win[None, None, :, :, None, :], bias, NEG_BIG)
    bias = bias.reshape(H, nblk, ROW_BLOCK, NCB, QW, kh * KB).transpose(1, 0, 2, 3, 4, 5)
    k5 = k.reshape(B, rows, GRID_W, H, Dh)
    v5 = v.reshape(B, rows, GRID_W, H, Dh)
    qb = q.reshape(B, nblk, ROW_BLOCK, NCB, QW, H, Dh).transpose(1, 0, 2, 3, 4, 5, 6)
    ridx = jnp.asarray(row_idx.reshape(nblk, ROW_BLOCK, kh), dtype=jnp.int32)
    cidx = jnp.asarray(col_idx, dtype=jnp.int32)
    nk = kh * KB
    scale = Dh ** -0.5

    def blk(args):
        qi, ri, bi = args
        sel = (slice(None), ri[:, None, :, None], cidx[None, :, None, :])
        kg = k5[sel].reshape(B, ROW_BLOCK, NCB, nk, H, Dh)
        vg = v5[sel].reshape(B, ROW_BLOCK, NCB, nk, H, Dh)
        s_loc = jnp.einsum('brnqhd,brnkhd->bhrnqk', qi, kg).astype(F32) * scale + bi[None]
        s_ctx = jnp.einsum('brnqhd,blhd->bhrnql', qi, k_ctx).astype(F32) * scale
        p = jax.nn.softmax(jnp.concatenate([s_loc, s_ctx], axis=-1), axis=-1).astype(v.dtype)
        return (jnp.einsum('bhrnqk,brnkhd->brnqhd', p[..., :nk], vg)
                + jnp.einsum('bhrnql,blhd->brnqhd', p[..., nk:], v_ctx))

    o = lax.map(blk, (qb, ridx, bias))
    return o.transpose(1, 0, 2, 3, 4, 5, 6).reshape(B, T, H, Dh)


def _clamped_swiglu(gu):
    g, u = gu[..., :D_FF], gu[..., D_FF:]
    g = jnp.minimum(g, SWIGLU_LIMIT)
    u = jnp.clip(u, -SWIGLU_LIMIT, SWIGLU_LIMIT)
    return g * jax.nn.sigmoid(SWIGLU_ALPHA * g) * (u + 1.0)


def _moe(h, w_router, b_router, w_up, b_up, w_down, b_down):
    T, D = h.shape
    logits = (h @ w_router).astype(F32) + b_router.astype(F32)
    top_val, top_idx = lax.top_k(logits, TOP_K)
    gates = jax.nn.softmax(top_val, axis=-1)
    TK = T * TOP_K
    flat_e = top_idx.reshape(TK)
    order = jnp.argsort(flat_e)
    sorted_e = flat_e[order]
    tok = order // TOP_K
    counts = jnp.bincount(flat_e, length=N_EXPERTS)
    starts = jnp.cumsum(counts) - counts
    padded = (counts + MOE_BLOCK - 1) // MOE_BLOCK * MOE_BLOCK
    pad_ends = jnp.cumsum(padded)
    pad_starts = pad_ends - padded
    dest = pad_starts[sorted_e] + jnp.arange(TK, dtype=jnp.int32) - starts[sorted_e]
    n_blocks = -(-TK // MOE_BLOCK) + N_EXPERTS
    L = n_blocks * MOE_BLOCK
    buf_tok = jnp.zeros((L,), jnp.int32).at[dest].set(tok)
    blk_e = jnp.minimum(jnp.searchsorted(pad_ends, jnp.arange(n_blocks) * MOE_BLOCK, side='right'),
                        N_EXPERTS - 1)
    xb = h[buf_tok].reshape(n_blocks, MOE_BLOCK, D)

    def expert_block(args):
        xe, e = args
        gu = xe @ w_up[e] + b_up[e]
        return _clamped_swiglu(gu) @ w_down[e] + b_down[e]

    yb = lax.map(expert_block, (xb, blk_e)).reshape(L, D)
    w_sorted = gates.reshape(TK)[order]
    out = jnp.zeros((T, D), F32).at[tok].add(w_sorted[:, None] * yb[dest].astype(F32))
    return out.astype(h.dtype)


def _ffn_sublayer(x, sh2, sc2, g2, lw):
    B, T, D = x.shape
    h = _rmsnorm(x, lw['norm2_g']) * (1 + sc2) + sh2
    y = _moe(h.reshape(B * T, D), lw['w_router'], lw['b_router'], lw['w_up'], lw['b_up'],
             lw['w_down'], lw['b_down']).reshape(B, T, D)
    return x + g2 * y


def _context_layer(x, mod, lw):
    sh1, sc1, g1, sh2, sc2, g2 = mod
    B, S, _ = x.shape
    h = _rmsnorm(x, lw['norm1_g']) * (1 + sc1) + sh1
    q_a, v_a, f_fw, f_bw, og_a, glu_b, q_c, k_c, v_c = _project(h, lw['w_in'])
    zero = jnp.zeros((B, H_A, HEAD_DIM, HEAD_DIM), F32)
    o_a, s_f, s_b = _hgrn2(q_a, v_a, f_fw, f_bw, og_a, lw['lb_f'], lw['lb_b'], lw['gn_g'], zero, zero)
    o_b = _conv_module(glu_b, lw)
    k_ctx = k_c.reshape(B, S, H_C, HEAD_DIM)
    v_ctx = v_c.reshape(B, S, H_C, HEAD_DIM)
    o_c = _context_attention(q_c.reshape(B, S, H_C, HEAD_DIM), k_ctx, v_ctx).reshape(B, S, C_W)
    x = x + g1 * (jnp.concatenate([o_a, o_b, o_c], axis=-1) @ lw['w_out'])
    x = _ffn_sublayer(x, sh2, sc2, g2, lw)
    return x, k_ctx, v_ctx, jnp.stack([s_f, s_b], axis=1).astype(x.dtype)


def _latent_layer(x, mod, lw, k_ctx, v_ctx, s0_f, s0_b):
    sh1, sc1, g1, sh2, sc2, g2 = mod
    B, T, _ = x.shape
    h = _rmsnorm(x, lw['norm1_g']) * (1 + sc1) + sh1
    q_a, v_a, f_fw, f_bw, og_a, glu_b, q_c, k_c, v_c = _project(h, lw['w_in'])
    o_a, _, _ = _hgrn2(q_a, v_a, f_fw, f_bw, og_a, lw['lb_f'], lw['lb_b'], lw['gn_g'], s0_f, s0_b)
    o_b = _conv_module(glu_b, lw)

    def heads(a):
        return a.reshape(B, T, H_C, HEAD_DIM)

    o_c = _na_latent(heads(q_c), heads(k_c), heads(v_c), k_ctx, v_ctx, lw['rpb']).reshape(B, T, C_W)
    x = x + g1 * (jnp.concatenate([o_a, o_b, o_c], axis=-1) @ lw['w_out'])
    return _ffn_sublayer(x, sh2, sc2, g2, lw)


def setup_inputs(seed: int = 0) -> dict:
    key = jax.random.key(seed)
    ks = jax.random.split(key, 29)

    def nrm(k, shape, scale):
        return jax.random.normal(k, shape, jnp.float32) * scale

    return {
        'x_prompt': nrm(ks[0], (BATCH, SEQ, D_MODEL), 1.0),
        'x_sample': nrm(ks[1], (DEC_BATCH, DEC_SEQ, D_MODEL), 1.0),
        'cache_na_k': nrm(ks[2], (DEC_BATCH, DEPTH, PAST_LEN, H_C, HEAD_DIM), 1.0),
        'cache_na_v': nrm(ks[3], (DEC_BATCH, DEPTH, PAST_LEN, H_C, HEAD_DIM), 1.0),
        'state_hgrn': nrm(ks[4], (DEC_BATCH, DEPTH, 2, H_A, HEAD_DIM, HEAD_DIM), 0.5),
        'c': nrm(ks[5], (DEC_BATCH, D_MODEL), 1.0),
        'c_ctx': nrm(ks[6], (D_MODEL,), 1.0),
        'w_mod': nrm(ks[7], (DEPTH, D_MODEL, 6 * D_MODEL), 0.5 * D_MODEL ** -0.5),
        'b_mod': nrm(ks[8], (DEPTH, 6 * D_MODEL), 0.02),
        'norm1_g': 1.0 + nrm(ks[9], (DEPTH, D_MODEL), 0.02),
        'norm2_g': 1.0 + nrm(ks[10], (DEPTH, D_MODEL), 0.02),
        'w_in': nrm(ks[11], (DEPTH, D_MODEL, IN_COLS), D_MODEL ** -0.5),
        'hgrn_lb': nrm(ks[12], (2, DEPTH, A_W), 0.5),
        'hgrn_gn_g': 1.0 + nrm(ks[13], (DEPTH, HEAD_DIM), 0.02),
        'conv_dw_w': nrm(ks[14], (DEPTH, CONV_W, B_W), CONV_W ** -0.5),
        'conv_dw_b': nrm(ks[15], (DEPTH, B_W), 0.02),
        'conv_ln_g': 1.0 + nrm(ks[16], (DEPTH, B_W), 0.02),
        'conv_ln_b': nrm(ks[17], (DEPTH, B_W), 0.02),
        'conv_pw_w': nrm(ks[18], (DEPTH, B_W, B_W), B_W ** -0.5),
        'conv_pw_b': nrm(ks[19], (DEPTH, B_W), 0.02),
        'na_rpb': nrm(ks[20], (DEPTH, H_C, 2 * KH_MAX - 1, 2 * KW - 1), 0.1),
        'w_out': nrm(ks[21], (DEPTH, MIX_W, D_MODEL), MIX_W ** -0.5),
        'w_router': nrm(ks[22], (DEPTH, D_MODEL, N_EXPERTS), D_MODEL ** -0.5),
        'b_router': nrm(ks[23], (DEPTH, N_EXPERTS), 0.01),
        'w_up': nrm(ks[24], (DEPTH, N_EXPERTS, D_MODEL, 2 * D_FF), D_MODEL ** -0.5),
        'b_up': nrm(ks[25], (DEPTH, N_EXPERTS, 2 * D_FF), 0.01),
        'w_down': nrm(ks[26], (DEPTH, N_EXPERTS, D_FF, D_MODEL), D_FF ** -0.5),
        'b_down': nrm(ks[27], (DEPTH, N_EXPERTS, D_MODEL), 0.01),
        'final_g': 1.0 + nrm(ks[28], (D_MODEL,), 0.02),
    }


def reference(x_prompt, x_sample, cache_na_k, cache_na_v, state_hgrn, c, c_ctx, w_mod, b_mod,
              norm1_g, norm2_g, w_in, hgrn_lb, hgrn_gn_g, conv_dw_w, conv_dw_b, conv_ln_g,
              conv_ln_b, conv_pw_w, conv_pw_b, na_rpb, w_out, w_router, b_router, w_up, b_up,
              w_down, b_down, final_g):
    lb_soft = jax.nn.softmax(hgrn_lb.astype(F32), axis=1)
    lower_bounds = jnp.cumsum(lb_soft, axis=1) - lb_soft[:, :1]
    y_p = x_prompt
    y_s = x_sample
    ks_new, vs_new, ss_new = [], [], []
    for l in range(DEPTH):
        lw = dict(norm1_g=norm1_g[l], norm2_g=norm2_g[l], w_in=w_in[l],
                  lb_f=lower_bounds[0, l], lb_b=lower_bounds[1, l], gn_g=hgrn_gn_g[l],
                  dw_w=conv_dw_w[l], dw_b=conv_dw_b[l], ln_g=conv_ln_g[l], ln_b=conv_ln_b[l],
                  pw_w=conv_pw_w[l], pw_b=conv_pw_b[l], rpb=na_rpb[l], w_out=w_out[l],
                  w_router=w_router[l], b_router=b_router[l], w_up=w_up[l], b_up=b_up[l],
                  w_down=w_down[l], b_down=b_down[l])
        mod_ctx = _modulation(c_ctx[None, None, :], w_mod[l], b_mod[l])
        y_p, k_l, v_l, s_l = _context_layer(y_p, mod_ctx, lw)
        ks_new.append(k_l)
        vs_new.append(v_l)
        ss_new.append(s_l)
        mod_lat = _modulation(c[:, None, :], w_mod[l], b_mod[l])
        y_s = _latent_layer(y_s, mod_lat, lw, cache_na_k[:, l], cache_na_v[:, l],
                            state_hgrn[:, l, 0], state_hgrn[:, l, 1])
    y_prompt = _rmsnorm(y_p, final_g)
    y_sample = _rmsnorm(y_s, final_g)
    new_cache_na_k = jnp.stack(ks_new, axis=1)
    new_cache_na_v = jnp.stack(vs_new, axis=1)
    new_state_hgrn = jnp.stack(ss_new, axis=1)
    return (y_prompt, y_sample, new_cache_na_k, new_cache_na_v, new_state_hgrn)
```

```python
import functools

import jax
import jax.numpy as jnp
import numpy as np
from jax import lax
from jax.experimental import pallas as pl
from jax.experimental.pallas import tpu as pltpu

D_MODEL = 2048
BATCH = 32
SEQ = 256
DEPTH = 4
DEC_BATCH = 4
DEC_SEQ = 2048
PAST_LEN = 256
GRID_W = 64
HEAD_DIM = 128
A_W = 3 * D_MODEL // 8
H_A = A_W // HEAD_DIM
B_W = D_MODEL // 4
C_W = D_MODEL - A_W - B_W
H_C = C_W // HEAD_DIM
MIX_W = A_W + B_W + C_W
IN_COLS = 5 * A_W + 2 * B_W + 3 * C_W
CHUNK = 16
CONV_W = 31
KH_MAX = 8
KW = 16
QW = 16
KB = QW + KW
NCB = GRID_W // QW
ROW_BLOCK = 2
Q_BLOCK = 128
N_EXPERTS = 32
TOP_K = 4
D_FF = D_MODEL
SWIGLU_ALPHA = 1.702
SWIGLU_LIMIT = 7.0
EPS = 1e-6
NEG_BIG = -1e30
LB_FLOOR = 1e-30
F32 = jnp.float32
BF16 = jnp.bfloat16

N_CTX = BATCH * SEQ
N_LAT = DEC_BATCH * DEC_SEQ
N_TOK = N_CTX + N_LAT
N_SEQ_ROWS = 1 + DEC_BATCH

V7X_VMEM_LIMIT_BYTES = 56 * 1024 * 1024

TM = 1024
TN = 1024
MOE_R = 1024
MOE_SUB = 256
MOE_TF = 256
MOE_L = N_TOK * TOP_K + N_EXPERTS * MOE_R
MOE_S = MOE_L // MOE_R


def _seq_row(i, tm):
    n_ctx_blocks = N_CTX // tm
    per_seq = DEC_SEQ // tm
    return jnp.where(i < n_ctx_blocks, 0, 1 + (i - n_ctx_blocks) // per_seq)


def _params(sem):
    return pltpu.CompilerParams(dimension_semantics=sem, vmem_limit_bytes=V7X_VMEM_LIMIT_BYTES)


def _mod_kernel(c_ref, w_ref, b_ref, o_ref):
    s = c_ref[...]
    s = s * jax.nn.sigmoid(s)
    o_ref[...] = jnp.dot(s.astype(BF16), w_ref[...].astype(BF16),
                         preferred_element_type=F32) + b_ref[...]


def _modulation_all(cond, w_mod, b_mod):
    tn = 1024
    n = 6 * D_MODEL
    return pl.pallas_call(
        _mod_kernel,
        grid=(DEPTH, n // tn),
        in_specs=[
            pl.BlockSpec((8, D_MODEL), lambda l, j: (0, 0)),
            pl.BlockSpec((None, D_MODEL, tn), lambda l, j: (l, 0, j)),
            pl.BlockSpec((None, 1, tn), lambda l, j: (l, 0, j)),
        ],
        out_specs=pl.BlockSpec((None, 8, tn), lambda l, j: (l, 0, j)),
        out_shape=jax.ShapeDtypeStruct((DEPTH, 8, n), F32),
        compiler_params=_params(("arbitrary", "arbitrary")),
        name="modulation",
    )(cond, w_mod, b_mod.reshape(DEPTH, 1, n))


def _rms(x, g):
    return x * lax.rsqrt(jnp.mean(x * x, axis=-1, keepdims=True) + EPS) * g


def _norm_mod_kernel(x_ref, g_ref, sc_ref, sh_ref, o_ref):
    y = _rms(x_ref[...], g_ref[...])
    o_ref[...] = (y * (1.0 + sc_ref[...]) + sh_ref[...]).astype(o_ref.dtype)


def _norm_mod(x, g, sc, sh):
    tm = 512
    return pl.pallas_call(
        _norm_mod_kernel,
        grid=(N_TOK // tm,),
        in_specs=[
            pl.BlockSpec((tm, D_MODEL), lambda i: (i, 0)),
            pl.BlockSpec((1, D_MODEL), lambda i: (0, 0)),
            pl.BlockSpec((None, 1, D_MODEL), lambda i: (_seq_row(i, tm), 0, 0)),
            pl.BlockSpec((None, 1, D_MODEL), lambda i: (_seq_row(i, tm), 0, 0)),
        ],
        out_specs=pl.BlockSpec((tm, D_MODEL), lambda i: (i, 0)),
        out_shape=jax.ShapeDtypeStruct((N_TOK, D_MODEL), BF16),
        compiler_params=_params(("arbitrary",)),
        name="norm_mod",
    )(x, g.reshape(1, D_MODEL), sc, sh)


def _norm_mod_router_kernel(x_ref, g_ref, sc_ref, sh_ref, wr_ref, br_ref, o_ref, lg_ref):
    y = _rms(x_ref[...], g_ref[...])
    h = y * (1.0 + sc_ref[...]) + sh_ref[...]
    o_ref[...] = h.astype(o_ref.dtype)
    lg_ref[...] = jnp.dot(h, wr_ref[...], preferred_element_type=F32,
                          precision=lax.Precision.HIGHEST) + br_ref[...]


def _norm_mod_router(x, g, sc, sh, w_router, b_router):
    tm = 512
    wr = jnp.zeros((D_MODEL, 128), F32).at[:, :N_EXPERTS].set(w_router)
    br = jnp.zeros((1, 128), F32).at[0, :N_EXPERTS].set(b_router)
    return pl.pallas_call(
        _norm_mod_router_kernel,
        grid=(N_TOK // tm,),
        in_specs=[
            pl.BlockSpec((tm, D_MODEL), lambda i: (i, 0)),
            pl.BlockSpec((1, D_MODEL), lambda i: (0, 0)),
            pl.BlockSpec((None, 1, D_MODEL), lambda i: (_seq_row(i, tm), 0, 0)),
            pl.BlockSpec((None, 1, D_MODEL), lambda i: (_seq_row(i, tm), 0, 0)),
            pl.BlockSpec((D_MODEL, 128), lambda i: (0, 0)),
            pl.BlockSpec((1, 128), lambda i: (0, 0)),
        ],
        out_specs=[pl.BlockSpec((tm, D_MODEL), lambda i: (i, 0)),
                   pl.BlockSpec((tm, 128), lambda i: (i, 0))],
        out_shape=[jax.ShapeDtypeStruct((N_TOK, D_MODEL), BF16),
                   jax.ShapeDtypeStruct((N_TOK, 128), F32)],
        compiler_params=_params(("arbitrary",)),
        name="norm_mod_router",
    )(x, g.reshape(1, D_MODEL), sc, sh, wr, br)


def _final_norm_kernel(x_ref, g_ref, o_ref):
    o_ref[...] = _rms(x_ref[...], g_ref[...])


def _final_norm(x, g):
    tm = 512
    return pl.pallas_call(
        _final_norm_kernel,
        grid=(N_TOK // tm,),
        in_specs=[pl.BlockSpec((tm, D_MODEL), lambda i: (i, 0)),
                  pl.BlockSpec((1, D_MODEL), lambda i: (0, 0))],
        out_specs=pl.BlockSpec((tm, D_MODEL), lambda i: (i, 0)),
        out_shape=jax.ShapeDtypeStruct((N_TOK, D_MODEL), F32),
        compiler_params=_params(("arbitrary",)),
        name="final_norm",
    )(x, g.reshape(1, D_MODEL))


def _proj_kernel(a_ref, w_ref, o_ref, wb_ref):
    @pl.when(pl.program_id(1) == 0)
    def _():
        wb_ref[...] = w_ref[...].astype(BF16)

    o_ref[...] = jnp.dot(a_ref[...], wb_ref[...], preferred_element_type=F32)


def _proj(a, w):
    k, n = w.shape
    return pl.pallas_call(
        _proj_kernel,
        grid=(n // TN, N_TOK // TM),
        in_specs=[pl.BlockSpec((TM, k), lambda j, i: (i, 0)),
                  pl.BlockSpec((k, TN), lambda j, i: (0, j))],
        out_specs=pl.BlockSpec((TM, TN), lambda j, i: (i, j)),
        out_shape=jax.ShapeDtypeStruct((N_TOK, n), F32),
        scratch_shapes=[pltpu.VMEM((k, TN), BF16)],
        compiler_params=_params(("arbitrary", "arbitrary")),
        name="proj",
    )(a, w)


def _proj_residual_kernel(a_ref, w_ref, x_ref, gate_ref, o_ref, wb_ref):
    @pl.when(pl.program_id(1) == 0)
    def _():
        wb_ref[...] = w_ref[...].astype(BF16)

    y = jnp.dot(a_ref[...], wb_ref[...], preferred_element_type=F32)
    o_ref[...] = x_ref[...] + gate_ref[...] * y


def _proj_residual(a, w, x, gate):
    k, n = w.shape
    return pl.pallas_call(
        _proj_residual_kernel,
        grid=(n // TN, N_TOK // TM),
        in_specs=[pl.BlockSpec((TM, k), lambda j, i: (i, 0)),
                  pl.BlockSpec((k, TN), lambda j, i: (0, j)),
                  pl.BlockSpec((TM, TN), lambda j, i: (i, j)),
                  pl.BlockSpec((None, 1, TN), lambda j, i: (_seq_row(i, TM), 0, j))],
        out_specs=pl.BlockSpec((TM, TN), lambda j, i: (i, j)),
        out_shape=jax.ShapeDtypeStruct((N_TOK, n), F32),
        scratch_shapes=[pltpu.VMEM((k, TN), BF16)],
        compiler_params=_params(("arbitrary", "arbitrary")),
        name="proj_residual",
    )(a, w, x, gate)


def _swiglu(g, u):
    g = jnp.minimum(g, SWIGLU_LIMIT)
    u = jnp.clip(u, -SWIGLU_LIMIT, SWIGLU_LIMIT)
    return g * jax.nn.sigmoid(SWIGLU_ALPHA * g) * (u + 1.0)


def _moe_kernel(blk_e_ref, nsub_ref, last_ref, xs_ref, wg_ref, wu_ref, wd_ref, bg_ref, bu_ref, bd_ref,
                o_ref, wgb_ref, wub_ref, wdb_ref):
    s = pl.program_id(0)
    f = pl.program_id(1)
    n = nsub_ref[s]

    def contribution(m):
        rows = pl.ds(pl.multiple_of(m * MOE_SUB, MOE_SUB), MOE_SUB)
        x = xs_ref[rows, :]
        g = jnp.dot(x, wgb_ref[...], preferred_element_type=F32) + bg_ref[...]
        u = jnp.dot(x, wub_ref[...], preferred_element_type=F32) + bu_ref[...]
        a = _swiglu(g, u).astype(BF16)
        return rows, jnp.dot(a, wdb_ref[...], preferred_element_type=F32)

    @pl.when(n > 0)
    def _():
        wgb_ref[...] = wg_ref[...].astype(BF16)
        wub_ref[...] = wu_ref[...].astype(BF16)
        wdb_ref[...] = wd_ref[...].astype(BF16)

        @pl.when(f == 0)
        def _():
            def body(m, carry):
                rows, y = contribution(m)
                o_ref[rows, :] = y + bd_ref[...]
                return carry
            lax.fori_loop(0, n, body, 0)

        @pl.when(f > 0)
        def _():
            def body(m, carry):
                rows, y = contribution(m)
                o_ref[rows, :] += y
                return carry
            lax.fori_loop(0, n, body, 0)


def _moe_experts(xs, blk_e, nsub, last_used, w_up, b_up, w_down, b_down):
    nf = D_FF // MOE_TF

    def blk(s, last):
        return jnp.minimum(s, last[0])

    grid_spec = pltpu.PrefetchScalarGridSpec(
        num_scalar_prefetch=3,
        grid=(MOE_S, nf),
        in_specs=[
            pl.BlockSpec((MOE_R, D_MODEL), lambda s, f, e, n, last: (blk(s, last), 0)),
            pl.BlockSpec((None, D_MODEL, MOE_TF),
                         lambda s, f, e, n, last: (e[s], 0, jnp.where(n[s] > 0, f, nf - 1))),
            pl.BlockSpec((None, D_MODEL, MOE_TF),
                         lambda s, f, e, n, last: (e[s], 0, nf + jnp.where(n[s] > 0, f, nf - 1))),
            pl.BlockSpec((None, MOE_TF, D_MODEL),
                         lambda s, f, e, n, last: (e[s], jnp.where(n[s] > 0, f, nf - 1), 0)),
            pl.BlockSpec((None, 1, MOE_TF),
                         lambda s, f, e, n, last: (e[s], 0, jnp.where(n[s] > 0, f, nf - 1))),
            pl.BlockSpec((None, 1, MOE_TF),
                         lambda s, f, e, n, last: (e[s], 0, nf + jnp.where(n[s] > 0, f, nf - 1))),
            pl.BlockSpec((None, 1, D_MODEL), lambda s, f, e, n, last: (e[s], 0, 0)),
        ],
        out_specs=pl.BlockSpec((MOE_R, D_MODEL), lambda s, f, e, n, last: (blk(s, last), 0)),
        scratch_shapes=[pltpu.VMEM((D_MODEL, MOE_TF), BF16),
                        pltpu.VMEM((D_MODEL, MOE_TF), BF16),
                        pltpu.VMEM((MOE_TF, D_MODEL), BF16)],
    )
    return pl.pallas_call(
        _moe_kernel,
        grid_spec=grid_spec,
        out_shape=jax.ShapeDtypeStruct((MOE_L, D_MODEL), F32),
        compiler_params=_params(("arbitrary", "arbitrary")),
        name="moe_experts",
    )(blk_e, nsub, last_used, xs, w_up, w_up, w_down,
      b_up.reshape(N_EXPERTS, 1, 2 * D_FF), b_up.reshape(N_EXPERTS, 1, 2 * D_FF),
      b_down.reshape(N_EXPERTS, 1, D_MODEL))


def _route(logits):
    top_val, top_idx = lax.top_k(logits, TOP_K)
    gates = jax.nn.softmax(top_val, axis=-1)
    flat_e = top_idx.reshape(-1)
    onehot = (flat_e[:, None] == jnp.arange(N_EXPERTS)[None, :]).astype(jnp.int32)
    rank = jnp.sum((jnp.cumsum(onehot, axis=0) - onehot) * onehot, axis=1)
    counts = jnp.sum(onehot, axis=0)
    padded = (counts + MOE_R - 1) // MOE_R * MOE_R
    pad_ends = jnp.cumsum(padded)
    pad_starts = pad_ends - padded
    dest = pad_starts[flat_e] + rank
    tok = jnp.arange(N_TOK * TOP_K, dtype=jnp.int32) // TOP_K
    buf_tok = jnp.zeros((MOE_L,), jnp.int32).at[dest].set(tok)
    blk_start = jnp.arange(MOE_S, dtype=jnp.int32) * MOE_R
    blk_e = jnp.minimum(jnp.searchsorted(pad_ends, blk_start, side='right'),
                        N_EXPERTS - 1).astype(jnp.int32)
    used = blk_start < pad_ends[-1]
    in_blk = jnp.clip(pad_starts[blk_e] + counts[blk_e] - blk_start, 0, MOE_R)
    nsub = jnp.where(used, (in_blk + MOE_SUB - 1) // MOE_SUB, 0).astype(jnp.int32)
    last_used = (pad_ends[-1] // MOE_R - 1).astype(jnp.int32).reshape(1)
    blk_e = jnp.where(used, blk_e, blk_e[last_used[0]])
    return gates, dest.reshape(N_TOK, TOP_K), buf_tok, blk_e, nsub, last_used


def _combine_kernel(x_ref, y_ref, w_ref, gate_ref, o_ref):
    w = w_ref[...]
    acc = w[:, 0:1] * y_ref[0]
    for k in range(1, TOP_K):
        acc = acc + w[:, k:k + 1] * y_ref[k]
    o_ref[...] = x_ref[...] + gate_ref[...] * acc


def _combine(x, yk, gates, gate2):
    tm = 256
    gp = jnp.zeros((N_TOK, 128), F32).at[:, :TOP_K].set(gates)
    return pl.pallas_call(
        _combine_kernel,
        grid=(N_TOK // tm,),
        in_specs=[pl.BlockSpec((tm, D_MODEL), lambda i: (i, 0)),
                  pl.BlockSpec((TOP_K, tm, D_MODEL), lambda i: (0, i, 0)),
                  pl.BlockSpec((tm, 128), lambda i: (i, 0)),
                  pl.BlockSpec((None, 1, D_MODEL), lambda i: (_seq_row(i, tm), 0, 0))],
        out_specs=pl.BlockSpec((tm, D_MODEL), lambda i: (i, 0)),
        out_shape=jax.ShapeDtypeStruct((N_TOK, D_MODEL), F32),
        compiler_params=_params(("arbitrary",)),
        name="moe_combine",
    )(x, yk, gp, gate2)


def _gla_chunked(q, k, v, log_f, s0):
    B, T, H, DK = q.shape
    DV = v.shape[-1]
    N = T // CHUNK

    def blocks(a):
        return a.astype(F32).reshape(B, N, CHUNK, H, a.shape[-1]).transpose(1, 0, 3, 2, 4)

    qc, kc, vc, gc = blocks(q), blocks(k), blocks(v), blocks(log_f)
    b = jnp.cumsum(gc, axis=3)
    b_last = b[:, :, :, -1, :]
    lower = np.tril(np.ones((CHUNK, CHUNK), dtype=bool))
    rel = jnp.where(lower[:, :, None], b[..., :, None, :] - b[..., None, :, :], NEG_BIG)
    scores = jnp.einsum('nbhtd,nbhsd,nbhtsd->nbhts', qc, kc, jnp.exp(rel))
    o_intra = jnp.einsum('nbhts,nbhse->nbhte', scores, vc)
    q_dec = qc * jnp.exp(b)
    k_dec = kc * jnp.exp(b_last[:, :, :, None, :] - b)

    def step(s, xs):
        qd, kd, vv, bl = xs
        o = jnp.einsum('bhtd,bhde->bhte', qd, s)
        s = s * jnp.exp(bl)[..., None] + jnp.einsum('bhtd,bhte->bhde', kd, vv)
        return s, o

    s_fin, o_inter = lax.scan(step, s0.astype(F32), (q_dec, k_dec, vc, b_last))
    o = (o_intra + o_inter).transpose(1, 0, 3, 2, 4).reshape(B, T, H, DV)
    return o, s_fin


def _hgrn2(q_raw, v_raw, zf_f, zf_b, og, lb_f, lb_b, gn_g, s0_f, s0_b):
    B, T, _ = q_raw.shape

    def heads(a):
        return a.reshape(B, T, H_A, HEAD_DIM)

    q = heads(jax.nn.silu(q_raw.astype(F32))) * HEAD_DIM ** -0.5
    v = heads(v_raw.astype(F32))

    def gates(z, lb):
        z = z.astype(F32)
        log_f = jnp.logaddexp(jnp.log(jnp.maximum(lb, LB_FLOOR)),
                              jnp.log1p(-lb) + jax.nn.log_sigmoid(z))
        k = (1.0 - lb) * jax.nn.sigmoid(-z)
        return heads(log_f), heads(k)

    lf_f, k_f = gates(zf_f, lb_f)
    lf_b, k_b = gates(zf_b, lb_b)
    o_f, s_f = _gla_chunked(q, k_f, v, lf_f, s0_f)

    def rev(a):
        return jnp.flip(a, axis=1)

    o_b, s_b = _gla_chunked(rev(q), rev(k_b), rev(v), rev(lf_b), s0_b)
    o = o_f + rev(o_b)
    o = o * lax.rsqrt(jnp.mean(o * o, axis=-1, keepdims=True) + EPS) * gn_g.astype(F32)
    o = o * jax.nn.silu(heads(og.astype(F32)))
    return o.reshape(B, T, A_W), s_f, s_b


def _conv_module(glu_in, lw):
    a, g = jnp.split(glu_in, 2, axis=-1)
    u = a * jax.nn.sigmoid(g)
    u = lax.conv_general_dilated(u, lw['dw_w'][:, None, :], (1,), [(CONV_W // 2, CONV_W // 2)],
                                 dimension_numbers=('NWC', 'WIO', 'NWC'),
                                 feature_group_count=B_W) + lw['dw_b']
    mu = jnp.mean(u, axis=-1, keepdims=True)
    var = jnp.mean(jnp.square(u - mu), axis=-1, keepdims=True)
    uf = (u - mu) * lax.rsqrt(var + EPS) * lw['ln_g'] + lw['ln_b']
    u = jax.nn.silu(uf)
    return u @ lw['pw_w'] + lw['pw_b']


def _context_attention(q, k, v):
    B, S, H, Dh = q.shape
    nq = S // Q_BLOCK
    qb = q.reshape(B, nq, Q_BLOCK, H, Dh).transpose(1, 0, 2, 3, 4)
    scale = Dh ** -0.5

    def blk(qi):
        s = jnp.einsum('bqhd,bkhd->bhqk', qi, k).astype(F32) * scale
        p = jax.nn.softmax(s, axis=-1).astype(v.dtype)
        return jnp.einsum('bhqk,bkhd->bqhd', p, v)

    o = lax.map(blk, qb)
    return o.transpose(1, 0, 2, 3, 4).reshape(B, S, H, Dh)


def _na_latent(q, k, v, k_ctx, v_ctx, rpb):
    B, T, H, Dh = q.shape
    rows = T // GRID_W
    kh = min(KH_MAX, rows)
    nblk = rows // ROW_BLOCK
    r = np.arange(rows)
    row_idx = np.clip(r - kh // 2, 0, rows - kh)[:, None] + np.arange(kh)[None, :]
    qcol = np.arange(GRID_W).reshape(NCB, QW)
    col_idx = np.clip(qcol[:, 0] - KW // 2, 0, GRID_W - KB)[:, None] + np.arange(KB)[None, :]
    win_start = np.clip(qcol - KW // 2, 0, GRID_W - KW)
    in_win = (col_idx[:, None, :] >= win_start[..., None]) & (col_idx[:, None, :] < win_start[..., None] + KW)
    dr = row_idx - r[:, None] + KH_MAX - 1
    dc = np.clip(col_idx[:, None, :] - qcol[..., None], -(KW - 1), KW - 1) + KW - 1
    bias = rpb.astype(F32)[:, dr[:, None, None, :, None], dc[None, :, :, None, :]]
    bias = jnp.where(in_win[None, None, :, :, None, :], bias, NEG_BIG)
    bias = bias.reshape(H, nblk, ROW_BLOCK, NCB, QW, kh * KB).transpose(1, 0, 2, 3, 4, 5)
    k5 = k.reshape(B, rows, GRID_W, H, Dh)
    v5 = v.reshape(B, rows, GRID_W, H, Dh)
    qb = q.reshape(B, nblk, ROW_BLOCK, NCB, QW, H, Dh).transpose(1, 0, 2, 3, 4, 5, 6)
    ridx = jnp.asarray(row_idx.reshape(nblk, ROW_BLOCK, kh), dtype=jnp.int32)
    cidx = jnp.asarray(col_idx, dtype=jnp.int32)
    nk = kh * KB
    scale = Dh ** -0.5

    def blk(args):
        qi, ri, bi = args
        sel = (slice(None), ri[:, None, :, None], cidx[None, :, None, :])
        kg = k5[sel].reshape(B, ROW_BLOCK, NCB, nk, H, Dh)
        vg = v5[sel].reshape(B, ROW_BLOCK, NCB, nk, H, Dh)
        s_loc = jnp.einsum('brnqhd,brnkhd->bhrnqk', qi, kg).astype(F32) * scale + bi[None]
        s_ctx = jnp.einsum('brnqhd,blhd->bhrnql', qi, k_ctx).astype(F32) * scale
        p = jax.nn.softmax(jnp.concatenate([s_loc, s_ctx], axis=-1), axis=-1).astype(v.dtype)
        return (jnp.einsum('bhrnqk,brnkhd->brnqhd', p[..., :nk], vg)
                + jnp.einsum('bhrnql,blhd->brnqhd', p[..., nk:], v_ctx))

    o = lax.map(blk, (qb, ridx, bias))
    return o.transpose(1, 0, 2, 3, 4, 5, 6).reshape(B, T, H, Dh)


_SPLITS = np.cumsum([A_W] * 5 + [2 * B_W] + [C_W] * 3)[:-1].tolist()


def _mixers(z, lw, cache_k, cache_v, s0_f, s0_b):
    zc = z[:N_CTX].reshape(BATCH, SEQ, IN_COLS)
    zl = z[N_CTX:].reshape(DEC_BATCH, DEC_SEQ, IN_COLS)

    q_a, v_a, f_fw, f_bw, og_a, glu_b, q_c, k_c, v_c = jnp.split(zc, _SPLITS, axis=-1)
    zero = jnp.zeros((BATCH, H_A, HEAD_DIM, HEAD_DIM), F32)
    o_a, s_f, s_b = _hgrn2(q_a, v_a, f_fw, f_bw, og_a, lw['lb_f'], lw['lb_b'], lw['gn_g'], zero, zero)
    o_b = _conv_module(glu_b, lw)
    k_ctx = k_c.reshape(BATCH, SEQ, H_C, HEAD_DIM)
    v_ctx = v_c.reshape(BATCH, SEQ, H_C, HEAD_DIM)
    o_c = _context_attention(q_c.reshape(BATCH, SEQ, H_C, HEAD_DIM), k_ctx, v_ctx).reshape(BATCH, SEQ, C_W)
    mix_c = jnp.concatenate([o_a, o_b, o_c], axis=-1).reshape(N_CTX, MIX_W)
    state = jnp.stack([s_f, s_b], axis=1)

    q_a, v_a, f_fw, f_bw, og_a, glu_b, q_c, k_c, v_c = jnp.split(zl, _SPLITS, axis=-1)
    o_a, _, _ = _hgrn2(q_a, v_a, f_fw, f_bw, og_a, lw['lb_f'], lw['lb_b'], lw['gn_g'], s0_f, s0_b)
    o_b = _conv_module(glu_b, lw)

    def heads(a):
        return a.reshape(DEC_BATCH, DEC_SEQ, H_C, HEAD_DIM)

    o_c = _na_latent(heads(q_c), heads(k_c), heads(v_c), cache_k, cache_v,
                     lw['rpb']).reshape(DEC_BATCH, DEC_SEQ, C_W)
    mix_l = jnp.concatenate([o_a, o_b, o_c], axis=-1).reshape(N_LAT, MIX_W)
    mix = jnp.concatenate([mix_c, mix_l], axis=0).astype(BF16)
    return mix, k_ctx, v_ctx, state


def kernel(x_prompt, x_sample, cache_na_k, cache_na_v, state_hgrn, c, c_ctx, w_mod, b_mod,
           norm1_g, norm2_g, w_in, hgrn_lb, hgrn_gn_g, conv_dw_w, conv_dw_b, conv_ln_g,
           conv_ln_b, conv_pw_w, conv_pw_b, na_rpb, w_out, w_router, b_router, w_up, b_up,
           w_down, b_down, final_g):
    lb_soft = jax.nn.softmax(hgrn_lb.astype(F32), axis=1)
    lower_bounds = jnp.cumsum(lb_soft, axis=1) - lb_soft[:, :1]

    cond = jnp.zeros((8, D_MODEL), F32).at[0].set(c_ctx).at[1:1 + DEC_BATCH].set(c)
    mods = _modulation_all(cond, w_mod, b_mod)
    mods = mods[:, :N_SEQ_ROWS].reshape(DEPTH, N_SEQ_ROWS, 6, 1, D_MODEL).transpose(0, 2, 1, 3, 4)

    x = jnp.concatenate([x_prompt.reshape(N_CTX, D_MODEL), x_sample.reshape(N_LAT, D_MODEL)], axis=0)
    ks_new, vs_new, ss_new = [], [], []
    for l in range(DEPTH):
        sh1, sc1, g1, sh2, sc2, g2 = (mods[l, i] for i in range(6))
        lw = dict(lb_f=lower_bounds[0, l], lb_b=lower_bounds[1, l], gn_g=hgrn_gn_g[l],
                  dw_w=conv_dw_w[l], dw_b=conv_dw_b[l], ln_g=conv_ln_g[l], ln_b=conv_ln_b[l],
                  pw_w=conv_pw_w[l], pw_b=conv_pw_b[l], rpb=na_rpb[l])
        h = _norm_mod(x, norm1_g[l], sc1, sh1)
        z = _proj(h, w_in[l])
        mix, k_l, v_l, s_l = _mixers(z, lw, cache_na_k[:, l], cache_na_v[:, l],
                                     state_hgrn[:, l, 0], state_hgrn[:, l, 1])
        ks_new.append(k_l)
        vs_new.append(v_l)
        ss_new.append(s_l)
        x = _proj_residual(mix, w_out[l], x, g1)

        h2, logits = _norm_mod_router(x, norm2_g[l], sc2, sh2, w_router[l], b_router[l])
        gates, dest, buf_tok, blk_e, nsub, last_used = _route(logits[:, :N_EXPERTS])
        xs = jnp.take(h2, buf_tok, axis=0)
        y = _moe_experts(xs, blk_e, nsub, last_used, w_up[l], b_up[l], w_down[l], b_down[l])
        yk = jnp.take(y, dest.T, axis=0)
        x = _combine(x, yk, gates, g2)

    y_all = _final_norm(x, final_g)
    y_prompt = y_all[:N_CTX].reshape(BATCH, SEQ, D_MODEL)
    y_sample = y_all[N_CTX:].reshape(DEC_BATCH, DEC_SEQ, D_MODEL)
    return (y_prompt, y_sample, jnp.stack(ks_new, axis=1), jnp.stack(vs_new, axis=1),
            jnp.stack(ss_new, axis=1))
```

```python
import functools

import jax
import jax.numpy as jnp
import numpy as np
from jax import lax
from jax.experimental import pallas as pl
from jax.experimental.pallas import tpu as pltpu

D_MODEL = 2048
BATCH = 32
SEQ = 256
DEPTH = 4
DEC_BATCH = 4
DEC_SEQ = 2048
PAST_LEN = 256
GRID_W = 64
HEAD_DIM = 128
A_W = 3 * D_MODEL // 8
H_A = A_W // HEAD_DIM
B_W = D_MODEL // 4
C_W = D_MODEL - A_W - B_W
H_C = C_W // HEAD_DIM
MIX_W = A_W + B_W + C_W
IN_COLS = 5 * A_W + 2 * B_W + 3 * C_W
CHUNK = 16
CONV_W = 31
KH_MAX = 8
KW = 16
QW = 16
KB = QW + KW
NCB = GRID_W // QW
ROW_BLOCK = 2
Q_BLOCK = 128
N_EXPERTS = 32
TOP_K = 4
D_FF = D_MODEL
SWIGLU_ALPHA = 1.702
SWIGLU_LIMIT = 7.0
EPS = 1e-6
NEG_BIG = -1e30
LB_FLOOR = 1e-30
F32 = jnp.float32
BF16 = jnp.bfloat16

N_CTX = BATCH * SEQ
N_LAT = DEC_BATCH * DEC_SEQ
N_TOK = N_CTX + N_LAT
N_SEQ_ROWS = 1 + DEC_BATCH

V7X_VMEM_LIMIT_BYTES = 56 * 1024 * 1024

TM = 1024
TN = 1024
MOE_R = 1024
MOE_SUB = 256
MOE_TF = 256
MOE_L = N_TOK * TOP_K + N_EXPERTS * MOE_R
MOE_S = MOE_L // MOE_R
DISPATCH_TOK = 256
COMBINE_TOK = 256


def _seq_row(i, tm):
    n_ctx_blocks = N_CTX // tm
    per_seq = DEC_SEQ // tm
    return jnp.where(i < n_ctx_blocks, 0, 1 + (i - n_ctx_blocks) // per_seq)


def _params(sem):
    return pltpu.CompilerParams(dimension_semantics=sem, vmem_limit_bytes=V7X_VMEM_LIMIT_BYTES)


def _mod_kernel(c_ref, w_ref, b_ref, o_ref):
    s = c_ref[...]
    s = s * jax.nn.sigmoid(s)
    o_ref[...] = jnp.dot(s.astype(BF16), w_ref[...].astype(BF16),
                         preferred_element_type=F32) + b_ref[...]


def _modulation_all(cond, w_mod, b_mod):
    tn = 1024
    n = 6 * D_MODEL
    return pl.pallas_call(
        _mod_kernel,
        grid=(DEPTH, n // tn),
        in_specs=[
            pl.BlockSpec((8, D_MODEL), lambda l, j: (0, 0)),
            pl.BlockSpec((None, D_MODEL, tn), lambda l, j: (l, 0, j)),
            pl.BlockSpec((None, 1, tn), lambda l, j: (l, 0, j)),
        ],
        out_specs=pl.BlockSpec((None, 8, tn), lambda l, j: (l, 0, j)),
        out_shape=jax.ShapeDtypeStruct((DEPTH, 8, n), F32),
        compiler_params=_params(("arbitrary", "arbitrary")),
        name="modulation",
    )(cond, w_mod, b_mod.reshape(DEPTH, 1, n))


def _rms(x, g):
    return x * lax.rsqrt(jnp.mean(x * x, axis=-1, keepdims=True) + EPS) * g


def _norm_mod_kernel(x_ref, g_ref, sc_ref, sh_ref, o_ref):
    y = _rms(x_ref[...], g_ref[...])
    o_ref[...] = (y * (1.0 + sc_ref[...]) + sh_ref[...]).astype(o_ref.dtype)


def _norm_mod(x, g, sc, sh, dtype):
    tm = 512
    return pl.pallas_call(
        _norm_mod_kernel,
        grid=(N_TOK // tm,),
        in_specs=[
            pl.BlockSpec((tm, D_MODEL), lambda i: (i, 0)),
            pl.BlockSpec((1, D_MODEL), lambda i: (0, 0)),
            pl.BlockSpec((None, 1, D_MODEL), lambda i: (_seq_row(i, tm), 0, 0)),
            pl.BlockSpec((None, 1, D_MODEL), lambda i: (_seq_row(i, tm), 0, 0)),
        ],
        out_specs=pl.BlockSpec((tm, D_MODEL), lambda i: (i, 0)),
        out_shape=jax.ShapeDtypeStruct((N_TOK, D_MODEL), dtype),
        compiler_params=_params(("arbitrary",)),
        name="norm_mod",
    )(x, g.reshape(1, D_MODEL), sc, sh)


def _norm_mod_router_kernel(x_ref, g_ref, sc_ref, sh_ref, wr_ref, br_ref, o_ref, lg_ref):
    y = _rms(x_ref[...], g_ref[...])
    h = y * (1.0 + sc_ref[...]) + sh_ref[...]
    o_ref[...] = h.astype(o_ref.dtype)
    lg_ref[...] = jnp.dot(h, wr_ref[...], preferred_element_type=F32,
                          precision=lax.Precision.HIGHEST) + br_ref[...]


def _norm_mod_router(x, g, sc, sh, w_router, b_router):
    tm = 512
    wr = jnp.zeros((D_MODEL, 128), F32).at[:, :N_EXPERTS].set(w_router)
    br = jnp.zeros((1, 128), F32).at[0, :N_EXPERTS].set(b_router)
    return pl.pallas_call(
        _norm_mod_router_kernel,
        grid=(N_TOK // tm,),
        in_specs=[
            pl.BlockSpec((tm, D_MODEL), lambda i: (i, 0)),
            pl.BlockSpec((1, D_MODEL), lambda i: (0, 0)),
            pl.BlockSpec((None, 1, D_MODEL), lambda i: (_seq_row(i, tm), 0, 0)),
            pl.BlockSpec((None, 1, D_MODEL), lambda i: (_seq_row(i, tm), 0, 0)),
            pl.BlockSpec((D_MODEL, 128), lambda i: (0, 0)),
            pl.BlockSpec((1, 128), lambda i: (0, 0)),
        ],
        out_specs=[pl.BlockSpec((tm, D_MODEL), lambda i: (i, 0)),
                   pl.BlockSpec((tm, 128), lambda i: (i, 0))],
        out_shape=[jax.ShapeDtypeStruct((N_TOK, D_MODEL), F32),
                   jax.ShapeDtypeStruct((N_TOK, 128), F32)],
        compiler_params=_params(("arbitrary",)),
        name="norm_mod_router",
    )(x, g.reshape(1, D_MODEL), sc, sh, wr, br)


def _final_norm_kernel(x_ref, g_ref, o_ref):
    o_ref[...] = _rms(x_ref[...], g_ref[...])


def _final_norm(x, g):
    tm = 512
    return pl.pallas_call(
        _final_norm_kernel,
        grid=(N_TOK // tm,),
        in_specs=[pl.BlockSpec((tm, D_MODEL), lambda i: (i, 0)),
                  pl.BlockSpec((1, D_MODEL), lambda i: (0, 0))],
        out_specs=pl.BlockSpec((tm, D_MODEL), lambda i: (i, 0)),
        out_shape=jax.ShapeDtypeStruct((N_TOK, D_MODEL), F32),
        compiler_params=_params(("arbitrary",)),
        name="final_norm",
    )(x, g.reshape(1, D_MODEL))


def _proj_kernel(a_ref, w_ref, o_ref, wb_ref):
    @pl.when(pl.program_id(1) == 0)
    def _():
        wb_ref[...] = w_ref[...].astype(BF16)

    o_ref[...] = jnp.dot(a_ref[...], wb_ref[...], preferred_element_type=F32)


def _proj(a, w):
    k, n = w.shape
    return pl.pallas_call(
        _proj_kernel,
        grid=(n // TN, N_TOK // TM),
        in_specs=[pl.BlockSpec((TM, k), lambda j, i: (i, 0)),
                  pl.BlockSpec((k, TN), lambda j, i: (0, j))],
        out_specs=pl.BlockSpec((TM, TN), lambda j, i: (i, j)),
        out_shape=jax.ShapeDtypeStruct((N_TOK, n), F32),
        scratch_shapes=[pltpu.VMEM((k, TN), BF16)],
        compiler_params=_params(("arbitrary", "arbitrary")),
        name="proj",
    )(a, w)


def _proj_residual_kernel(a_ref, w_ref, x_ref, gate_ref, o_ref, wb_ref):
    @pl.when(pl.program_id(1) == 0)
    def _():
        wb_ref[...] = w_ref[...].astype(BF16)

    y = jnp.dot(a_ref[...], wb_ref[...], preferred_element_type=F32)
    o_ref[...] = x_ref[...] + gate_ref[...] * y


def _proj_residual(a, w, x, gate):
    k, n = w.shape
    return pl.pallas_call(
        _proj_residual_kernel,
        grid=(n // TN, N_TOK // TM),
        in_specs=[pl.BlockSpec((TM, k), lambda j, i: (i, 0)),
                  pl.BlockSpec((k, TN), lambda j, i: (0, j)),
                  pl.BlockSpec((TM, TN), lambda j, i: (i, j)),
                  pl.BlockSpec((None, 1, TN), lambda j, i: (_seq_row(i, TM), 0, j))],
        out_specs=pl.BlockSpec((TM, TN), lambda j, i: (i, j)),
        out_shape=jax.ShapeDtypeStruct((N_TOK, n), F32),
        scratch_shapes=[pltpu.VMEM((k, TN), BF16)],
        compiler_params=_params(("arbitrary", "arbitrary")),
        name="proj_residual",
    )(a, w, x, gate)


def _swiglu(g, u):
    g = jnp.minimum(g, SWIGLU_LIMIT)
    u = jnp.clip(u, -SWIGLU_LIMIT, SWIGLU_LIMIT)
    return g * jax.nn.sigmoid(SWIGLU_ALPHA * g) * (u + 1.0)


def _moe_kernel(blk_e_ref, nsub_ref, last_ref, xs_ref, wg_ref, wu_ref, wd_ref, bg_ref, bu_ref, bd_ref,
                o_ref, wgb_ref, wub_ref, wdb_ref):
    s = pl.program_id(0)
    f = pl.program_id(1)
    n = nsub_ref[s]

    def contribution(m):
        rows = pl.ds(pl.multiple_of(m * MOE_SUB, MOE_SUB), MOE_SUB)
        x = xs_ref[rows, :].astype(BF16)
        g = jnp.dot(x, wgb_ref[...], preferred_element_type=F32) + bg_ref[...]
        u = jnp.dot(x, wub_ref[...], preferred_element_type=F32) + bu_ref[...]
        a = _swiglu(g, u).astype(BF16)
        return rows, jnp.dot(a, wdb_ref[...], preferred_element_type=F32)

    @pl.when(n > 0)
    def _():
        wgb_ref[...] = wg_ref[...].astype(BF16)
        wub_ref[...] = wu_ref[...].astype(BF16)
        wdb_ref[...] = wd_ref[...].astype(BF16)

        @pl.when(f == 0)
        def _():
            def body(m, carry):
                rows, y = contribution(m)
                o_ref[rows, :] = y + bd_ref[...]
                return carry
            lax.fori_loop(0, n, body, 0)

        @pl.when(f > 0)
        def _():
            def body(m, carry):
                rows, y = contribution(m)
                o_ref[rows, :] += y
                return carry
            lax.fori_loop(0, n, body, 0)


def _moe_experts(xs, blk_e, nsub, last_used, w_up, b_up, w_down, b_down):
    nf = D_FF // MOE_TF

    def blk(s, last):
        return jnp.minimum(s, last[0])

    grid_spec = pltpu.PrefetchScalarGridSpec(
        num_scalar_prefetch=3,
        grid=(MOE_S, nf),
        in_specs=[
            pl.BlockSpec((MOE_R, D_MODEL), lambda s, f, e, n, last: (blk(s, last), 0)),
            pl.BlockSpec((None, D_MODEL, MOE_TF),
                         lambda s, f, e, n, last: (e[s], 0, jnp.where(n[s] > 0, f, nf - 1))),
            pl.BlockSpec((None, D_MODEL, MOE_TF),
                         lambda s, f, e, n, last: (e[s], 0, nf + jnp.where(n[s] > 0, f, nf - 1))),
            pl.BlockSpec((None, MOE_TF, D_MODEL),
                         lambda s, f, e, n, last: (e[s], jnp.where(n[s] > 0, f, nf - 1), 0)),
            pl.BlockSpec((None, 1, MOE_TF),
                         lambda s, f, e, n, last: (e[s], 0, jnp.where(n[s] > 0, f, nf - 1))),
            pl.BlockSpec((None, 1, MOE_TF),
                         lambda s, f, e, n, last: (e[s], 0, nf + jnp.where(n[s] > 0, f, nf - 1))),
            pl.BlockSpec((None, 1, D_MODEL), lambda s, f, e, n, last: (e[s], 0, 0)),
        ],
        out_specs=pl.BlockSpec((MOE_R, D_MODEL), lambda s, f, e, n, last: (blk(s, last), 0)),
        scratch_shapes=[pltpu.VMEM((D_MODEL, MOE_TF), BF16),
                        pltpu.VMEM((D_MODEL, MOE_TF), BF16),
                        pltpu.VMEM((MOE_TF, D_MODEL), BF16)],
    )
    return pl.pallas_call(
        _moe_kernel,
        grid_spec=grid_spec,
        out_shape=jax.ShapeDtypeStruct((MOE_L, D_MODEL), F32),
        compiler_params=_params(("arbitrary", "arbitrary")),
        name="moe_experts",
    )(blk_e, nsub, last_used, xs, w_up, w_up, w_down,
      b_up.reshape(N_EXPERTS, 1, 2 * D_FF), b_up.reshape(N_EXPERTS, 1, 2 * D_FF),
      b_down.reshape(N_EXPERTS, 1, D_MODEL))


def _route(logits):
    top_val, top_idx = lax.top_k(logits, TOP_K)
    gates = jax.nn.softmax(top_val, axis=-1)
    flat_e = top_idx.reshape(-1)
    onehot = (flat_e[:, None] == jnp.arange(N_EXPERTS)[None, :]).astype(jnp.int32)
    rank = jnp.sum((jnp.cumsum(onehot, axis=0) - onehot) * onehot, axis=1)
    counts = jnp.sum(onehot, axis=0)
    padded = (counts + MOE_R - 1) // MOE_R * MOE_R
    pad_ends = jnp.cumsum(padded)
    pad_starts = pad_ends - padded
    dest = (pad_starts[flat_e] + rank).astype(jnp.int32)
    blk_start = jnp.arange(MOE_S, dtype=jnp.int32) * MOE_R
    blk_e = jnp.minimum(jnp.searchsorted(pad_ends, blk_start, side='right'),
                        N_EXPERTS - 1).astype(jnp.int32)
    used = blk_start < pad_ends[-1]
    in_blk = jnp.clip(pad_starts[blk_e] + counts[blk_e] - blk_start, 0, MOE_R)
    nsub = jnp.where(used, (in_blk + MOE_SUB - 1) // MOE_SUB, 0).astype(jnp.int32)
    last_used = (pad_ends[-1] // MOE_R - 1).astype(jnp.int32).reshape(1)
    blk_e = jnp.where(used, blk_e, blk_e[last_used[0]])
    first_pad_row = ((pad_starts + counts) // 8 * 8).astype(jnp.int32)
    return gates, dest, first_pad_row, blk_e, nsub, last_used


def _row_copy(src_hbm, src_row, dst, dst_row, sem):
    return pltpu.make_async_copy(src_hbm.at[pl.ds(src_row, 1)], dst.at[pl.ds(dst_row, 1)], sem)


def _dispatch_kernel(pad_row_ref, dest_ref, h_hbm, xs_hbm, zero_ref, zero_sem, row_sem):
    i = pl.program_id(0)
    n_steps = pl.num_programs(0)

    def zero_fill(e):
        row = pl.multiple_of(pad_row_ref[e], 8)
        return pltpu.make_async_copy(zero_ref, xs_hbm.at[pl.ds(row, MOE_SUB)], zero_sem)

    @pl.when(i == 0)
    def _():
        zero_ref[...] = jnp.zeros_like(zero_ref)
        for e in range(N_EXPERTS):
            zero_fill(e).start()
        for e in range(N_EXPERTS):
            zero_fill(e).wait()

    tok0 = i * DISPATCH_TOK

    def issue(j, carry):
        _row_copy(h_hbm, tok0 + j // TOP_K, xs_hbm, dest_ref[0, j], row_sem).start()
        return carry

    lax.fori_loop(0, DISPATCH_TOK * TOP_K, issue, 0, unroll=8)

    def wait_step():
        pltpu.make_async_copy(h_hbm.at[pl.ds(0, DISPATCH_TOK * TOP_K)],
                              xs_hbm.at[pl.ds(0, DISPATCH_TOK * TOP_K)], row_sem).wait()

    @pl.when(i > 0)
    def _():
        wait_step()

    @pl.when(i == n_steps - 1)
    def _():
        wait_step()


def _dispatch(h, dest, first_pad_row):
    n_steps = N_TOK // DISPATCH_TOK
    grid_spec = pltpu.PrefetchScalarGridSpec(
        num_scalar_prefetch=1,
        grid=(n_steps,),
        in_specs=[
            pl.BlockSpec((None, 1, DISPATCH_TOK * TOP_K), lambda i, p: (i, 0, 0),
                         memory_space=pltpu.SMEM),
            pl.BlockSpec(memory_space=pl.ANY),
        ],
        out_specs=pl.BlockSpec(memory_space=pl.ANY),
        scratch_shapes=[pltpu.VMEM((MOE_SUB, D_MODEL), F32),
                        pltpu.SemaphoreType.DMA(()),
                        pltpu.SemaphoreType.DMA(())],
    )
    return pl.pallas_call(
        _dispatch_kernel,
        grid_spec=grid_spec,
        out_shape=jax.ShapeDtypeStruct((MOE_L + MOE_SUB, D_MODEL), F32),
        compiler_params=_params(("arbitrary",)),
        name="moe_dispatch",
    )(first_pad_row, dest.reshape(n_steps, 1, DISPATCH_TOK * TOP_K), h)


def _combine_kernel(dest_ref, dest_next_ref, x_ref, y_hbm, w_ref, gate_ref, o_ref, buf_ref, sems):
    i = pl.program_id(0)
    n_steps = pl.num_programs(0)
    slot = i % 2

    def issue(d_ref, to_slot):
        for k in range(TOP_K):
            def body(t, carry):
                _row_copy(y_hbm, d_ref[0, k * COMBINE_TOK + t], buf_ref.at[to_slot, k], t,
                          sems.at[to_slot]).start()
                return carry
            lax.fori_loop(0, COMBINE_TOK, body, 0, unroll=8)

    @pl.when(i == 0)
    def _():
        issue(dest_ref, 0)

    @pl.when(i + 1 < n_steps)
    def _():
        issue(dest_next_ref, 1 - slot)

    for k in range(TOP_K):
        pltpu.make_async_copy(y_hbm.at[pl.ds(0, COMBINE_TOK)], buf_ref.at[slot, k], sems.at[slot]).wait()

    w = w_ref[...]
    acc = w[:, 0:1] * buf_ref[slot, 0]
    for k in range(1, TOP_K):
        acc = acc + w[:, k:k + 1] * buf_ref[slot, k]
    o_ref[...] = x_ref[...] + gate_ref[...] * acc


def _combine(x, y, dest, gates, gate2):
    tm = COMBINE_TOK
    n_steps = N_TOK // tm
    gp = jnp.zeros((N_TOK, 128), F32).at[:, :TOP_K].set(gates)
    dest_km = dest.reshape(n_steps, tm, TOP_K).transpose(0, 2, 1).reshape(n_steps, 1, TOP_K * tm)
    return pl.pallas_call(
        _combine_kernel,
        grid=(n_steps,),
        in_specs=[pl.BlockSpec((None, 1, TOP_K * tm), lambda i: (i, 0, 0), memory_space=pltpu.SMEM),
                  pl.BlockSpec((None, 1, TOP_K * tm), lambda i: (jnp.minimum(i + 1, n_steps - 1), 0, 0),
                               memory_space=pltpu.SMEM),
                  pl.BlockSpec((tm, D_MODEL), lambda i: (i, 0)),
                  pl.BlockSpec(memory_space=pl.ANY),
                  pl.BlockSpec((tm, 128), lambda i: (i, 0)),
                  pl.BlockSpec((None, 1, D_MODEL), lambda i: (_seq_row(i, tm), 0, 0))],
        out_specs=pl.BlockSpec((tm, D_MODEL), lambda i: (i, 0)),
        out_shape=jax.ShapeDtypeStruct((N_TOK, D_MODEL), F32),
        scratch_shapes=[pltpu.VMEM((2, TOP_K, tm, D_MODEL), F32),
                        pltpu.SemaphoreType.DMA((2,))],
        compiler_params=_params(("arbitrary",)),
        name="moe_combine",
    )(dest_km, dest_km, x, y, gp, gate2)


HG_T = 256
HG_LEVELS = 8
N_SEG = N_TOK // HG_T
CTX_SEGS = N_CTX // HG_T
LAT_SEGS_PER_SEQ = DEC_SEQ // HG_T
N_SEQ = BATCH + DEC_BATCH


def _hgrn_tables(reverse):
    t = np.arange(HG_T)
    r, j = t[:, None], t[None, :]
    sums = np.zeros((HG_LEVELS + 2, HG_T, HG_T), np.float32)
    pairs = np.zeros((HG_LEVELS + 1, HG_T, HG_T), np.float32)
    for l in range(HG_LEVELS):
        same_block = (r >> l) == (j >> l)
        later_half = ((r >> l) & 1) == 1
        sums[l] = same_block & np.where(later_half, j <= r, j > r)
        pairs[l] = ((r >> (l + 1)) == (j >> (l + 1))) & later_half & (((j >> l) & 1) == 0)
    sums[HG_LEVELS] = j <= r
    sums[HG_LEVELS + 1] = j > r
    pairs[HG_LEVELS] = r == j
    if reverse:
        sums = sums[:, ::-1, ::-1]
        pairs = pairs[:, ::-1, ::-1]
    return jnp.asarray(sums, BF16), jnp.asarray(pairs, F32)


def _dot_nt(x, y):
    return lax.dot_general(x, y, (((1,), (1,)), ((), ())), preferred_element_type=F32)


def _dot_tn(x, y):
    return lax.dot_general(x, y, (((0,), (0,)), ((), ())), preferred_element_type=F32)


def _hgrn_segment(g, reverse):
    seg = (N_SEG - 1 - g) if reverse else g
    lat = seg - CTX_SEGS
    first = (LAT_SEGS_PER_SEQ - 1) if reverse else 0
    starts = jnp.logical_or(seg < CTX_SEGS, lax.rem(lat, LAT_SEGS_PER_SEQ) == first)
    seq = jnp.where(seg < CTX_SEGS, seg, CTX_SEGS + lat // LAT_SEGS_PER_SEQ)
    return seg, seq, starts


def _hgrn_kernel(q_ref, v_ref, z_ref, la_ref, lc_ref, om_ref, sums_ref, pairs_ref, s0_ref,
                 o_ref, sfin_ref, st_ref, *, reverse):
    _, _, starts = _hgrn_segment(pl.program_id(1), reverse)

    @pl.when(starts)
    def _():
        st_ref[...] = s0_ref[...]

    qr = q_ref[...]
    q = qr * jax.nn.sigmoid(qr) * HEAD_DIM ** -0.5
    z = z_ref[...]
    log_sig = jnp.minimum(z, 0.0) - jnp.log1p(jnp.exp(-jnp.abs(z)))
    a = la_ref[...]
    b = lc_ref[...] + log_sig
    log_f = jnp.maximum(a, b) + jnp.log1p(jnp.exp(-jnp.abs(a - b)))
    k = om_ref[...] * jax.nn.sigmoid(-z)
    v = v_ref[...].astype(BF16)
    hi = log_f.astype(BF16)
    lo = (log_f - hi.astype(F32)).astype(BF16)
    parts = jnp.concatenate([hi, lo], axis=1)

    def block_sum(i):
        s2 = jnp.dot(sums_ref[i], parts, preferred_element_type=F32)
        return s2[:, :HEAD_DIM] + s2[:, HEAD_DIM:]

    scores = pairs_ref[HG_LEVELS] * _dot_nt(q.astype(BF16), k.astype(BF16))
    for l in range(HG_LEVELS):
        e = jnp.exp(block_sum(l))
        scores = scores + pairs_ref[l] * _dot_nt((q * e).astype(BF16), (k * e).astype(BF16))
    o = jnp.dot(scores.astype(BF16), v, preferred_element_type=F32)

    state = st_ref[...]
    q_dec = (q * jnp.exp(block_sum(HG_LEVELS))).astype(BF16)
    o_ref[...] = o + jnp.dot(q_dec, state.astype(BF16), preferred_element_type=F32)

    k_dec = (k * jnp.exp(block_sum(HG_LEVELS + 1))).astype(BF16)
    total = _dot_tn(parts, jnp.ones((HG_T, HEAD_DIM), BF16))
    decay = jnp.exp(total[:HEAD_DIM] + total[HEAD_DIM:])
    new_state = state * decay + _dot_tn(k_dec, v)
    st_ref[...] = new_state
    sfin_ref[...] = new_state


def _hgrn_scan(z, la, lc, om, s0_all, reverse):
    sums, pairs = _hgrn_tables(reverse)
    gate_col = (3 if reverse else 2) * H_A

    def seg_of(g):
        return _hgrn_segment(g, reverse)[0]

    def seq_of(g):
        return _hgrn_segment(g, reverse)[1]

    row_block = lambda col0: pl.BlockSpec((HG_T, HEAD_DIM), lambda h, g: (seg_of(g), col0 + h))
    chan = pl.BlockSpec((1, HEAD_DIM), lambda h, g: (0, h))
    state_block = pl.BlockSpec((None, None, HEAD_DIM, HEAD_DIM), lambda h, g: (seq_of(g), h, 0, 0))
    return pl.pallas_call(
        functools.partial(_hgrn_kernel, reverse=reverse),
        grid=(H_A, N_SEG),
        in_specs=[row_block(0), row_block(H_A), row_block(gate_col), chan, chan, chan,
                  pl.BlockSpec(sums.shape, lambda h, g: (0, 0, 0)),
                  pl.BlockSpec(pairs.shape, lambda h, g: (0, 0, 0)),
                  state_block],
        out_specs=[pl.BlockSpec((HG_T, HEAD_DIM), lambda h, g: (seg_of(g), h)), state_block],
        out_shape=[jax.ShapeDtypeStruct((N_TOK, A_W), F32),
                   jax.ShapeDtypeStruct((N_SEQ, H_A, HEAD_DIM, HEAD_DIM), F32)],
        scratch_shapes=[pltpu.VMEM((HEAD_DIM, HEAD_DIM), F32)],
        compiler_params=_params(("arbitrary", "arbitrary")),
        name="hgrn_scan_bwd" if reverse else "hgrn_scan_fwd",
    )(z, z, z, la, lc, om, sums, pairs, s0_all)


def _hgrn_finish_kernel(of_ref, ob_ref, og_ref, gn_ref, o_ref):
    o = of_ref[...] + ob_ref[...]
    og = og_ref[...]
    for h in range(H_A):
        cols = slice(h * HEAD_DIM, (h + 1) * HEAD_DIM)
        oh = o[:, cols]
        gh = og[:, cols]
        y = oh * lax.rsqrt(jnp.mean(oh * oh, axis=-1, keepdims=True) + EPS) * gn_ref[...]
        o_ref[:, cols] = (y * (gh * jax.nn.sigmoid(gh))).astype(o_ref.dtype)


def _hgrn_finish(o_f, o_b, z, gn_g):
    tm = 512
    blk = pl.BlockSpec((tm, A_W), lambda i: (i, 0))
    return pl.pallas_call(
        _hgrn_finish_kernel,
        grid=(N_TOK // tm,),
        in_specs=[blk, blk,
                  pl.BlockSpec((tm, A_W), lambda i: (i, 4)),
                  pl.BlockSpec((1, HEAD_DIM), lambda i: (0, 0))],
        out_specs=blk,
        out_shape=jax.ShapeDtypeStruct((N_TOK, A_W), BF16),
        compiler_params=_params(("arbitrary",)),
        name="hgrn_finish",
    )(o_f, o_b, z, gn_g.reshape(1, HEAD_DIM))


def _hgrn2(z, lb_f, lb_b, gn_g, state_l):
    outs, finals = [], []
    for reverse, lb in ((False, lb_f), (True, lb_b)):
        la = jnp.log(jnp.maximum(lb, LB_FLOOR)).reshape(1, A_W)
        lc = jnp.log1p(-lb).reshape(1, A_W)
        om = (1.0 - lb).reshape(1, A_W)
        s0_all = jnp.concatenate(
            [jnp.zeros((BATCH, H_A, HEAD_DIM, HEAD_DIM), F32), state_l[:, int(reverse)]], axis=0)
        o, sfin = _hgrn_scan(z, la, lc, om, s0_all, reverse)
        outs.append(o)
        finals.append(sfin[:BATCH])
    return _hgrn_finish(outs[0], outs[1], z, gn_g), jnp.stack(finals, axis=1)


def _conv_module(glu_in, lw):
    a, g = jnp.split(glu_in, 2, axis=-1)
    u = a * jax.nn.sigmoid(g)
    u = lax.conv_general_dilated(u, lw['dw_w'][:, None, :], (1,), [(CONV_W // 2, CONV_W // 2)],
                                 dimension_numbers=('NWC', 'WIO', 'NWC'),
                                 feature_group_count=B_W) + lw['dw_b']
    mu = jnp.mean(u, axis=-1, keepdims=True)
    var = jnp.mean(jnp.square(u - mu), axis=-1, keepdims=True)
    uf = (u - mu) * lax.rsqrt(var + EPS) * lw['ln_g'] + lw['ln_b']
    u = jax.nn.silu(uf)
    return u @ lw['pw_w'] + lw['pw_b']


def _context_attention(q, k, v):
    B, S, H, Dh = q.shape
    nq = S // Q_BLOCK
    qb = q.reshape(B, nq, Q_BLOCK, H, Dh).transpose(1, 0, 2, 3, 4)
    scale = Dh ** -0.5

    def blk(qi):
        s = jnp.einsum('bqhd,bkhd->bhqk', qi, k).astype(F32) * scale
        p = jax.nn.softmax(s, axis=-1).astype(v.dtype)
        return jnp.einsum('bhqk,bkhd->bqhd', p, v)

    o = lax.map(blk, qb)
    return o.transpose(1, 0, 2, 3, 4).reshape(B, S, H, Dh)


def _na_latent(q, k, v, k_ctx, v_ctx, rpb):
    B, T, H, Dh = q.shape
    rows = T // GRID_W
    kh = min(KH_MAX, rows)
    nblk = rows // ROW_BLOCK
    r = np.arange(rows)
    row_idx = np.clip(r - kh // 2, 0, rows - kh)[:, None] + np.arange(kh)[None, :]
    qcol = np.arange(GRID_W).reshape(NCB, QW)
    col_idx = np.clip(qcol[:, 0] - KW // 2, 0, GRID_W - KB)[:, None] + np.arange(KB)[None, :]
    win_start = np.clip(qcol - KW // 2, 0, GRID_W - KW)
    in_win = (col_idx[:, None, :] >= win_start[..., None]) & (col_idx[:, None, :] < win_start[..., None] + KW)
    dr = row_idx - r[:, None] + KH_MAX - 1
    dc = np.clip(col_idx[:, None, :] - qcol[..., None], -(KW - 1), KW - 1) + KW - 1
    bias = rpb.astype(F32)[:, dr[:, None, None, :, None], dc[None, :, :, None, :]]
    bias = jnp.where(in_win[None, None, :, :, None, :], bias, NEG_BIG)
    bias = bias.reshape(H, nblk, ROW_BLOCK, NCB, QW, kh * KB).transpose(1, 0, 2, 3, 4, 5)
    k5 = k.reshape(B, rows, GRID_W, H, Dh)
    v5 = v.reshape(B, rows, GRID_W, H, Dh)
    qb = q.reshape(B, nblk, ROW_BLOCK, NCB, QW, H, Dh).transpose(1, 0, 2, 3, 4, 5, 6)
    ridx = jnp.asarray(row_idx.reshape(nblk, ROW_BLOCK, kh), dtype=jnp.int32)
    cidx = jnp.asarray(col_idx, dtype=jnp.int32)
    nk = kh * KB
    scale = Dh ** -0.5

    def blk(args):
        qi, ri, bi = args
        sel = (slice(None), ri[:, None, :, None], cidx[None, :, None, :])
        kg = k5[sel].reshape(B, ROW_BLOCK, NCB, nk, H, Dh)
        vg = v5[sel].reshape(B, ROW_BLOCK, NCB, nk, H, Dh)
        s_loc = jnp.einsum('brnqhd,brnkhd->bhrnqk', qi, kg).astype(F32) * scale + bi[None]
        s_ctx = jnp.einsum('brnqhd,blhd->bhrnql', qi, k_ctx).astype(F32) * scale
        p = jax.nn.softmax(jnp.concatenate([s_loc, s_ctx], axis=-1), axis=-1).astype(v.dtype)
        return (jnp.einsum('bhrnqk,brnkhd->brnqhd', p[..., :nk], vg)
                + jnp.einsum('bhrnql,blhd->brnqhd', p[..., nk:], v_ctx))

    o = lax.map(blk, (qb, ridx, bias))
    return o.transpose(1, 0, 2, 3, 4, 5, 6).reshape(B, T, H, Dh)


def _mixers(z, lw, cache_k, cache_v, state_l):
    o_a, state = _hgrn2(z, lw['lb_f'], lw['lb_b'], lw['gn_g'], state_l)
    rest = z[:, 5 * A_W:]
    zc = rest[:N_CTX].reshape(BATCH, SEQ, IN_COLS - 5 * A_W)
    zl = rest[N_CTX:].reshape(DEC_BATCH, DEC_SEQ, IN_COLS - 5 * A_W)
    splits = [2 * B_W, 2 * B_W + C_W, 2 * B_W + 2 * C_W]

    glu_b, q_c, k_c, v_c = jnp.split(zc, splits, axis=-1)
    o_b = _conv_module(glu_b, lw)
    k_ctx = k_c.reshape(BATCH, SEQ, H_C, HEAD_DIM)
    v_ctx = v_c.reshape(BATCH, SEQ, H_C, HEAD_DIM)
    o_c = _context_attention(q_c.reshape(BATCH, SEQ, H_C, HEAD_DIM), k_ctx, v_ctx).reshape(BATCH, SEQ, C_W)
    mix_c = jnp.concatenate([o_b, o_c], axis=-1).reshape(N_CTX, B_W + C_W)

    glu_b, q_c, k_c, v_c = jnp.split(zl, splits, axis=-1)
    o_b = _conv_module(glu_b, lw)

    def heads(a):
        return a.reshape(DEC_BATCH, DEC_SEQ, H_C, HEAD_DIM)

    o_c = _na_latent(heads(q_c), heads(k_c), heads(v_c), cache_k, cache_v,
                     lw['rpb']).reshape(DEC_BATCH, DEC_SEQ, C_W)
    mix_l = jnp.concatenate([o_b, o_c], axis=-1).reshape(N_LAT, B_W + C_W)
    mix_bc = jnp.concatenate([mix_c, mix_l], axis=0).astype(BF16)
    return jnp.concatenate([o_a, mix_bc], axis=-1), k_ctx, v_ctx, state


def kernel(x_prompt, x_sample, cache_na_k, cache_na_v, state_hgrn, c, c_ctx, w_mod, b_mod,
           norm1_g, norm2_g, w_in, hgrn_lb, hgrn_gn_g, conv_dw_w, conv_dw_b, conv_ln_g,
           conv_ln_b, conv_pw_w, conv_pw_b, na_rpb, w_out, w_router, b_router, w_up, b_up,
           w_down, b_down, final_g):
    lb_soft = jax.nn.softmax(hgrn_lb.astype(F32), axis=1)
    lower_bounds = jnp.cumsum(lb_soft, axis=1) - lb_soft[:, :1]

    cond = jnp.zeros((8, D_MODEL), F32).at[0].set(c_ctx).at[1:1 + DEC_BATCH].set(c)
    mods = _modulation_all(cond, w_mod, b_mod)
    mods = mods[:, :N_SEQ_ROWS].reshape(DEPTH, N_SEQ_ROWS, 6, 1, D_MODEL).transpose(0, 2, 1, 3, 4)

    x = jnp.concatenate([x_prompt.reshape(N_CTX, D_MODEL), x_sample.reshape(N_LAT, D_MODEL)], axis=0)
    ks_new, vs_new, ss_new = [], [], []
    for l in range(DEPTH):
        sh1, sc1, g1, sh2, sc2, g2 = (mods[l, i] for i in range(6))
        lw = dict(lb_f=lower_bounds[0, l], lb_b=lower_bounds[1, l], gn_g=hgrn_gn_g[l],
                  dw_w=conv_dw_w[l], dw_b=conv_dw_b[l], ln_g=conv_ln_g[l], ln_b=conv_ln_b[l],
                  pw_w=conv_pw_w[l], pw_b=conv_pw_b[l], rpb=na_rpb[l])
        h = _norm_mod(x, norm1_g[l], sc1, sh1, BF16)
        z = _proj(h, w_in[l])
        mix, k_l, v_l, s_l = _mixers(z, lw, cache_na_k[:, l], cache_na_v[:, l], state_hgrn[:, l])
        ks_new.append(k_l)
        vs_new.append(v_l)
        ss_new.append(s_l)
        x = _proj_residual(mix, w_out[l], x, g1)

        h2, logits = _norm_mod_router(x, norm2_g[l], sc2, sh2, w_router[l], b_router[l])
        gates, dest, first_pad_row, blk_e, nsub, last_used = _route(logits[:, :N_EXPERTS])
        xs = _dispatch(h2, dest, first_pad_row)
        y = _moe_experts(xs, blk_e, nsub, last_used, w_up[l], b_up[l], w_down[l], b_down[l])
        x = _combine(x, y, dest, gates, g2)

    y_all = _final_norm(x, final_g)
    y_prompt = y_all[:N_CTX].reshape(BATCH, SEQ, D_MODEL)
    y_sample = y_all[N_CTX:].reshape(DEC_BATCH, DEC_SEQ, D_MODEL)
    return (y_prompt, y_sample, jnp.stack(ks_new, axis=1), jnp.stack(vs_new, axis=1),
            jnp.stack(ss_new, axis=1))
```

```python
import functools

import jax
import jax.numpy as jnp
import numpy as np
from jax import lax
from jax.experimental import pallas as pl
from jax.experimental.pallas import tpu as pltpu

D_MODEL = 2048
BATCH = 32
SEQ = 256
DEPTH = 4
DEC_BATCH = 4
DEC_SEQ = 2048
PAST_LEN = 256
GRID_W = 64
HEAD_DIM = 128
A_W = 3 * D_MODEL // 8
H_A = A_W // HEAD_DIM
B_W = D_MODEL // 4
C_W = D_MODEL - A_W - B_W
H_C = C_W // HEAD_DIM
MIX_W = A_W + B_W + C_W
IN_COLS = 5 * A_W + 2 * B_W + 3 * C_W
CHUNK = 16
CONV_W = 31
KH_MAX = 8
KW = 16
QW = 16
KB = QW + KW
NCB = GRID_W // QW
ROW_BLOCK = 2
Q_BLOCK = 128
N_EXPERTS = 32
TOP_K = 4
D_FF = D_MODEL
SWIGLU_ALPHA = 1.702
SWIGLU_LIMIT = 7.0
EPS = 1e-6
NEG_BIG = -1e30
LB_FLOOR = 1e-30
F32 = jnp.float32
BF16 = jnp.bfloat16

N_CTX = BATCH * SEQ
N_LAT = DEC_BATCH * DEC_SEQ
N_TOK = N_CTX + N_LAT
N_SEQ_ROWS = 1 + DEC_BATCH

V7X_VMEM_LIMIT_BYTES = 56 * 1024 * 1024
V7X_VMEM_LIMIT_MOE_BYTES = 60 * 1024 * 1024

TM = 1024
TN = 1024
MOE_R = 1024
MOE_SUB = 256
MOE_TF = 512
MOE_L = N_TOK * TOP_K + N_EXPERTS * MOE_R
MOE_S = MOE_L // MOE_R
DISPATCH_TOK = 256
COMBINE_TOK = 256


def _seq_row(i, tm):
    n_ctx_blocks = N_CTX // tm
    per_seq = DEC_SEQ // tm
    return jnp.where(i < n_ctx_blocks, 0, 1 + (i - n_ctx_blocks) // per_seq)


def _params(sem, vmem_limit_bytes=V7X_VMEM_LIMIT_BYTES):
    return pltpu.CompilerParams(dimension_semantics=sem, vmem_limit_bytes=vmem_limit_bytes)


def _mod_kernel(c_ref, w_ref, b_ref, o_ref):
    s = c_ref[...]
    s = s * jax.nn.sigmoid(s)
    o_ref[...] = jnp.dot(s.astype(BF16), w_ref[...].astype(BF16),
                         preferred_element_type=F32) + b_ref[...]


def _modulation_all(cond, w_mod, b_mod):
    tn = 1024
    n = 6 * D_MODEL
    return pl.pallas_call(
        _mod_kernel,
        grid=(DEPTH, n // tn),
        in_specs=[
            pl.BlockSpec((8, D_MODEL), lambda l, j: (0, 0)),
            pl.BlockSpec((None, D_MODEL, tn), lambda l, j: (l, 0, j)),
            pl.BlockSpec((None, 1, tn), lambda l, j: (l, 0, j)),
        ],
        out_specs=pl.BlockSpec((None, 8, tn), lambda l, j: (l, 0, j)),
        out_shape=jax.ShapeDtypeStruct((DEPTH, 8, n), F32),
        compiler_params=_params(("arbitrary", "arbitrary")),
        name="modulation",
    )(cond, w_mod, b_mod.reshape(DEPTH, 1, n))


def _rms(x, g):
    return x * lax.rsqrt(jnp.mean(x * x, axis=-1, keepdims=True) + EPS) * g


def _norm_mod_kernel(x_ref, g_ref, sc_ref, sh_ref, o_ref):
    y = _rms(x_ref[...], g_ref[...])
    o_ref[...] = (y * (1.0 + sc_ref[...]) + sh_ref[...]).astype(o_ref.dtype)


def _norm_mod(x, g, sc, sh, dtype):
    tm = 512
    return pl.pallas_call(
        _norm_mod_kernel,
        grid=(N_TOK // tm,),
        in_specs=[
            pl.BlockSpec((tm, D_MODEL), lambda i: (i, 0)),
            pl.BlockSpec((1, D_MODEL), lambda i: (0, 0)),
            pl.BlockSpec((None, 1, D_MODEL), lambda i: (_seq_row(i, tm), 0, 0)),
            pl.BlockSpec((None, 1, D_MODEL), lambda i: (_seq_row(i, tm), 0, 0)),
        ],
        out_specs=pl.BlockSpec((tm, D_MODEL), lambda i: (i, 0)),
        out_shape=jax.ShapeDtypeStruct((N_TOK, D_MODEL), dtype),
        compiler_params=_params(("arbitrary",)),
        name="norm_mod",
    )(x, g.reshape(1, D_MODEL), sc, sh)


def _norm_mod_router_kernel(x_ref, g_ref, sc_ref, sh_ref, wr_ref, br_ref, o_ref, lg_ref):
    y = _rms(x_ref[...], g_ref[...])
    h = y * (1.0 + sc_ref[...]) + sh_ref[...]
    o_ref[...] = h.astype(o_ref.dtype)
    lg_ref[...] = jnp.dot(h, wr_ref[...], preferred_element_type=F32,
                          precision=lax.Precision.HIGHEST) + br_ref[...]


def _norm_mod_router(x, g, sc, sh, w_router, b_router):
    tm = 512
    wr = jnp.zeros((D_MODEL, 128), F32).at[:, :N_EXPERTS].set(w_router)
    br = jnp.zeros((1, 128), F32).at[0, :N_EXPERTS].set(b_router)
    return pl.pallas_call(
        _norm_mod_router_kernel,
        grid=(N_TOK // tm,),
        in_specs=[
            pl.BlockSpec((tm, D_MODEL), lambda i: (i, 0)),
            pl.BlockSpec((1, D_MODEL), lambda i: (0, 0)),
            pl.BlockSpec((None, 1, D_MODEL), lambda i: (_seq_row(i, tm), 0, 0)),
            pl.BlockSpec((None, 1, D_MODEL), lambda i: (_seq_row(i, tm), 0, 0)),
            pl.BlockSpec((D_MODEL, 128), lambda i: (0, 0)),
            pl.BlockSpec((1, 128), lambda i: (0, 0)),
        ],
        out_specs=[pl.BlockSpec((tm, D_MODEL), lambda i: (i, 0)),
                   pl.BlockSpec((tm, 128), lambda i: (i, 0))],
        out_shape=[jax.ShapeDtypeStruct((N_TOK, D_MODEL), F32),
                   jax.ShapeDtypeStruct((N_TOK, 128), F32)],
        compiler_params=_params(("arbitrary",)),
        name="norm_mod_router",
    )(x, g.reshape(1, D_MODEL), sc, sh, wr, br)


def _final_norm_kernel(x_ref, g_ref, o_ref):
    o_ref[...] = _rms(x_ref[...], g_ref[...])


def _final_norm(x, g):
    tm = 512
    return pl.pallas_call(
        _final_norm_kernel,
        grid=(N_TOK // tm,),
        in_specs=[pl.BlockSpec((tm, D_MODEL), lambda i: (i, 0)),
                  pl.BlockSpec((1, D_MODEL), lambda i: (0, 0))],
        out_specs=pl.BlockSpec((tm, D_MODEL), lambda i: (i, 0)),
        out_shape=jax.ShapeDtypeStruct((N_TOK, D_MODEL), F32),
        compiler_params=_params(("arbitrary",)),
        name="final_norm",
    )(x, g.reshape(1, D_MODEL))


def _proj_kernel(a_ref, w_ref, o_ref, wb_ref):
    @pl.when(pl.program_id(1) == 0)
    def _():
        wb_ref[...] = w_ref[...].astype(BF16)

    o_ref[...] = jnp.dot(a_ref[...], wb_ref[...], preferred_element_type=F32)


def _proj(a, w):
    k, n = w.shape
    return pl.pallas_call(
        _proj_kernel,
        grid=(n // TN, N_TOK // TM),
        in_specs=[pl.BlockSpec((TM, k), lambda j, i: (i, 0)),
                  pl.BlockSpec((k, TN), lambda j, i: (0, j))],
        out_specs=pl.BlockSpec((TM, TN), lambda j, i: (i, j)),
        out_shape=jax.ShapeDtypeStruct((N_TOK, n), F32),
        scratch_shapes=[pltpu.VMEM((k, TN), BF16)],
        compiler_params=_params(("arbitrary", "arbitrary")),
        name="proj",
    )(a, w)


def _proj_residual_kernel(a_ref, w_ref, x_ref, gate_ref, o_ref, wb_ref):
    @pl.when(pl.program_id(1) == 0)
    def _():
        wb_ref[...] = w_ref[...].astype(BF16)

    y = jnp.dot(a_ref[...], wb_ref[...], preferred_element_type=F32)
    o_ref[...] = x_ref[...] + gate_ref[...] * y


def _proj_residual(a, w, x, gate):
    k, n = w.shape
    return pl.pallas_call(
        _proj_residual_kernel,
        grid=(n // TN, N_TOK // TM),
        in_specs=[pl.BlockSpec((TM, k), lambda j, i: (i, 0)),
                  pl.BlockSpec((k, TN), lambda j, i: (0, j)),
                  pl.BlockSpec((TM, TN), lambda j, i: (i, j)),
                  pl.BlockSpec((None, 1, TN), lambda j, i: (_seq_row(i, TM), 0, j))],
        out_specs=pl.BlockSpec((TM, TN), lambda j, i: (i, j)),
        out_shape=jax.ShapeDtypeStruct((N_TOK, n), F32),
        scratch_shapes=[pltpu.VMEM((k, TN), BF16)],
        compiler_params=_params(("arbitrary", "arbitrary")),
        name="proj_residual",
    )(a, w, x, gate)


def _swiglu(g, u):
    g = jnp.minimum(g, SWIGLU_LIMIT)
    u = jnp.clip(u, -SWIGLU_LIMIT, SWIGLU_LIMIT)
    return g * jax.nn.sigmoid(SWIGLU_ALPHA * g) * (u + 1.0)


def _moe_kernel(blk_e_ref, nsub_ref, last_ref, xs_ref, wg_ref, wu_ref, wd_ref, bg_ref, bu_ref, bd_ref,
                o_ref, wgb_ref, wub_ref, wdb_ref):
    s = pl.program_id(0)
    f = pl.program_id(1)
    n = nsub_ref[s]

    def contribution(m):
        rows = pl.ds(pl.multiple_of(m * MOE_SUB, MOE_SUB), MOE_SUB)
        x = xs_ref[rows, :]
        g = jnp.dot(x, wgb_ref[...], preferred_element_type=F32) + bg_ref[...]
        u = jnp.dot(x, wub_ref[...], preferred_element_type=F32) + bu_ref[...]
        a = _swiglu(g, u).astype(BF16)
        return rows, jnp.dot(a, wdb_ref[...], preferred_element_type=F32)

    @pl.when(n > 0)
    def _():
        wgb_ref[...] = wg_ref[...].astype(BF16)
        wub_ref[...] = wu_ref[...].astype(BF16)
        wdb_ref[...] = wd_ref[...].astype(BF16)

        @pl.when(f == 0)
        def _():
            def body(m, carry):
                rows, y = contribution(m)
                o_ref[rows, :] = y + bd_ref[...]
                return carry
            lax.fori_loop(0, n, body, 0)

        @pl.when(f > 0)
        def _():
            def body(m, carry):
                rows, y = contribution(m)
                o_ref[rows, :] += y
                return carry
            lax.fori_loop(0, n, body, 0)


def _moe_experts(xs, blk_e, nsub, last_used, w_up, b_up, w_down, b_down):
    nf = D_FF // MOE_TF

    def blk(s, last):
        return jnp.minimum(s, last[0])

    grid_spec = pltpu.PrefetchScalarGridSpec(
        num_scalar_prefetch=3,
        grid=(MOE_S, nf),
        in_specs=[
            pl.BlockSpec((MOE_R, D_MODEL), lambda s, f, e, n, last: (blk(s, last), 0)),
            pl.BlockSpec((None, D_MODEL, MOE_TF),
                         lambda s, f, e, n, last: (e[s], 0, jnp.where(n[s] > 0, f, nf - 1))),
            pl.BlockSpec((None, D_MODEL, MOE_TF),
                         lambda s, f, e, n, last: (e[s], 0, nf + jnp.where(n[s] > 0, f, nf - 1))),
            pl.BlockSpec((None, MOE_TF, D_MODEL),
                         lambda s, f, e, n, last: (e[s], jnp.where(n[s] > 0, f, nf - 1), 0)),
            pl.BlockSpec((None, 1, MOE_TF),
                         lambda s, f, e, n, last: (e[s], 0, jnp.where(n[s] > 0, f, nf - 1))),
            pl.BlockSpec((None, 1, MOE_TF),
                         lambda s, f, e, n, last: (e[s], 0, nf + jnp.where(n[s] > 0, f, nf - 1))),
            pl.BlockSpec((None, 1, D_MODEL), lambda s, f, e, n, last: (e[s], 0, 0)),
        ],
        out_specs=pl.BlockSpec((MOE_R, D_MODEL), lambda s, f, e, n, last: (blk(s, last), 0)),
        scratch_shapes=[pltpu.VMEM((D_MODEL, MOE_TF), BF16),
                        pltpu.VMEM((D_MODEL, MOE_TF), BF16),
                        pltpu.VMEM((MOE_TF, D_MODEL), BF16)],
    )
    return pl.pallas_call(
        _moe_kernel,
        grid_spec=grid_spec,
        out_shape=jax.ShapeDtypeStruct((MOE_L, D_MODEL), F32),
        compiler_params=_params(("arbitrary", "arbitrary"), V7X_VMEM_LIMIT_MOE_BYTES),
        name="moe_experts",
    )(blk_e, nsub, last_used, xs, w_up, w_up, w_down,
      b_up.reshape(N_EXPERTS, 1, 2 * D_FF), b_up.reshape(N_EXPERTS, 1, 2 * D_FF),
      b_down.reshape(N_EXPERTS, 1, D_MODEL))


def _route(logits):
    top_val, top_idx = lax.top_k(logits, TOP_K)
    gates = jax.nn.softmax(top_val, axis=-1)
    flat_e = top_idx.reshape(-1)
    onehot = (flat_e[:, None] == jnp.arange(N_EXPERTS)[None, :]).astype(jnp.int32)
    rank = jnp.sum((jnp.cumsum(onehot, axis=0) - onehot) * onehot, axis=1)
    counts = jnp.sum(onehot, axis=0)
    padded = (counts + MOE_R - 1) // MOE_R * MOE_R
    pad_ends = jnp.cumsum(padded)
    pad_starts = pad_ends - padded
    dest = (pad_starts[flat_e] + rank).astype(jnp.int32)
    blk_start = jnp.arange(MOE_S, dtype=jnp.int32) * MOE_R
    blk_e = jnp.minimum(jnp.searchsorted(pad_ends, blk_start, side='right'),
                        N_EXPERTS - 1).astype(jnp.int32)
    used = blk_start < pad_ends[-1]
    in_blk = jnp.clip(pad_starts[blk_e] + counts[blk_e] - blk_start, 0, MOE_R)
    nsub = jnp.where(used, (in_blk + MOE_SUB - 1) // MOE_SUB, 0).astype(jnp.int32)
    last_used = (pad_ends[-1] // MOE_R - 1).astype(jnp.int32).reshape(1)
    blk_e = jnp.where(used, blk_e, blk_e[last_used[0]])
    tok = jnp.arange(N_TOK * TOP_K, dtype=jnp.int32) // TOP_K
    src_tok = jnp.zeros((MOE_L,), jnp.int32).at[dest].set(tok)
    return gates, dest, src_tok, blk_e, nsub, last_used


def _row_copy(src_hbm, src_row, dst, dst_row, sem):
    return pltpu.make_async_copy(src_hbm.at[pl.ds(src_row, 1)], dst.at[pl.ds(dst_row, 1)], sem)


ROW_ISSUE_UNROLL = 8


def _dispatch_kernel(nsub_ref, src_ref, src_next_ref, h_hbm, o_ref, buf_ref, sems):
    i = pl.program_id(0)
    n_steps = pl.num_programs(0)
    slot = i % 2

    def issue(step, s_ref, to_slot):
        def body(c, carry):
            for u in range(ROW_ISSUE_UNROLL):
                r = c * ROW_ISSUE_UNROLL + u
                _row_copy(h_hbm, s_ref[0, r], buf_ref.at[to_slot], r, sems.at[to_slot]).start()
            return carry
        lax.fori_loop(0, nsub_ref[step] * (MOE_SUB // ROW_ISSUE_UNROLL), body, 0)

    @pl.when(i == 0)
    def _():
        issue(0, src_ref, 0)

    @pl.when(i + 1 < n_steps)
    def _():
        issue(i + 1, src_next_ref, 1 - slot)

    for g in range(MOE_R // MOE_SUB):
        @pl.when(g < nsub_ref[i])
        def _():
            pltpu.make_async_copy(h_hbm.at[pl.ds(0, MOE_SUB)], buf_ref.at[slot, pl.ds(0, MOE_SUB)],
                                  sems.at[slot]).wait()

    for g in range(MOE_R // MOE_SUB):
        @pl.when(g < nsub_ref[i])
        def _():
            rows = pl.ds(g * MOE_SUB, MOE_SUB)
            o_ref[rows, :] = buf_ref[slot, rows, :].astype(o_ref.dtype)


def _dispatch(h, src_tok, nsub):
    grid_spec = pltpu.PrefetchScalarGridSpec(
        num_scalar_prefetch=1,
        grid=(MOE_S,),
        in_specs=[
            pl.BlockSpec((None, 1, MOE_R), lambda i, n: (i, 0, 0), memory_space=pltpu.SMEM),
            pl.BlockSpec((None, 1, MOE_R), lambda i, n: (jnp.minimum(i + 1, MOE_S - 1), 0, 0),
                         memory_space=pltpu.SMEM),
            pl.BlockSpec(memory_space=pl.ANY),
        ],
        out_specs=pl.BlockSpec((MOE_R, D_MODEL), lambda i, n: (i, 0)),
        scratch_shapes=[pltpu.VMEM((2, MOE_R, D_MODEL), F32),
                        pltpu.SemaphoreType.DMA((2,))],
    )
    src = src_tok.reshape(MOE_S, 1, MOE_R)
    return pl.pallas_call(
        _dispatch_kernel,
        grid_spec=grid_spec,
        out_shape=jax.ShapeDtypeStruct((MOE_L, D_MODEL), BF16),
        compiler_params=_params(("arbitrary",)),
        name="moe_dispatch",
    )(nsub, src, src, h)


def _combine_kernel(dest_ref, dest_next_ref, x_ref, y_hbm, w_ref, gate_ref, o_ref, buf_ref, sems):
    i = pl.program_id(0)
    n_steps = pl.num_programs(0)
    slot = i % 2

    def issue(d_ref, to_slot):
        for k in range(TOP_K):
            def body(t, carry):
                _row_copy(y_hbm, d_ref[0, k * COMBINE_TOK + t], buf_ref.at[to_slot, k], t,
                          sems.at[to_slot]).start()
                return carry
            lax.fori_loop(0, COMBINE_TOK, body, 0, unroll=8)

    @pl.when(i == 0)
    def _():
        issue(dest_ref, 0)

    @pl.when(i + 1 < n_steps)
    def _():
        issue(dest_next_ref, 1 - slot)

    for k in range(TOP_K):
        pltpu.make_async_copy(y_hbm.at[pl.ds(0, COMBINE_TOK)], buf_ref.at[slot, k], sems.at[slot]).wait()

    w = w_ref[...]
    acc = w[:, 0:1] * buf_ref[slot, 0]
    for k in range(1, TOP_K):
        acc = acc + w[:, k:k + 1] * buf_ref[slot, k]
    o_ref[...] = x_ref[...] + gate_ref[...] * acc


def _combine(x, y, dest, gates, gate2):
    tm = COMBINE_TOK
    n_steps = N_TOK // tm
    gp = jnp.zeros((N_TOK, 128), F32).at[:, :TOP_K].set(gates)
    dest_km = dest.reshape(n_steps, tm, TOP_K).transpose(0, 2, 1).reshape(n_steps, 1, TOP_K * tm)
    return pl.pallas_call(
        _combine_kernel,
        grid=(n_steps,),
        in_specs=[pl.BlockSpec((None, 1, TOP_K * tm), lambda i: (i, 0, 0), memory_space=pltpu.SMEM),
                  pl.BlockSpec((None, 1, TOP_K * tm), lambda i: (jnp.minimum(i + 1, n_steps - 1), 0, 0),
                               memory_space=pltpu.SMEM),
                  pl.BlockSpec((tm, D_MODEL), lambda i: (i, 0)),
                  pl.BlockSpec(memory_space=pl.ANY),
                  pl.BlockSpec((tm, 128), lambda i: (i, 0)),
                  pl.BlockSpec((None, 1, D_MODEL), lambda i: (_seq_row(i, tm), 0, 0))],
        out_specs=pl.BlockSpec((tm, D_MODEL), lambda i: (i, 0)),
        out_shape=jax.ShapeDtypeStruct((N_TOK, D_MODEL), F32),
        scratch_shapes=[pltpu.VMEM((2, TOP_K, tm, D_MODEL), F32),
                        pltpu.SemaphoreType.DMA((2,))],
        compiler_params=_params(("arbitrary",)),
        name="moe_combine",
    )(dest_km, dest_km, x, y, gp, gate2)


HG_T = 256
HG_LEVELS = 8
N_SEG = N_TOK // HG_T
CTX_SEGS = N_CTX // HG_T
LAT_SEGS_PER_SEQ = DEC_SEQ // HG_T
N_SEQ = BATCH + DEC_BATCH


def _hgrn_tables(reverse):
    t = np.arange(HG_T)
    r, j = t[:, None], t[None, :]
    sums = np.zeros((HG_LEVELS + 2, HG_T, HG_T), np.float32)
    pairs = np.zeros((HG_LEVELS + 1, HG_T, HG_T), np.float32)
    for l in range(HG_LEVELS):
        same_block = (r >> l) == (j >> l)
        later_half = ((r >> l) & 1) == 1
        sums[l] = same_block & np.where(later_half, j <= r, j > r)
        pairs[l] = ((r >> (l + 1)) == (j >> (l + 1))) & later_half & (((j >> l) & 1) == 0)
    sums[HG_LEVELS] = j <= r
    sums[HG_LEVELS + 1] = j > r
    pairs[HG_LEVELS] = r == j
    if reverse:
        sums = sums[:, ::-1, ::-1]
        pairs = pairs[:, ::-1, ::-1]
    return jnp.asarray(sums, BF16), jnp.asarray(pairs, F32)


def _dot_nt(x, y):
    return lax.dot_general(x, y, (((1,), (1,)), ((), ())), preferred_element_type=F32)


def _dot_tn(x, y):
    return lax.dot_general(x, y, (((0,), (0,)), ((), ())), preferred_element_type=F32)


def _hgrn_segment(g, reverse):
    seg = (N_SEG - 1 - g) if reverse else g
    lat = seg - CTX_SEGS
    first = (LAT_SEGS_PER_SEQ - 1) if reverse else 0
    starts = jnp.logical_or(seg < CTX_SEGS, lax.rem(lat, LAT_SEGS_PER_SEQ) == first)
    seq = jnp.where(seg < CTX_SEGS, seg, CTX_SEGS + lat // LAT_SEGS_PER_SEQ)
    return seg, seq, starts


def _hgrn_kernel(q_ref, v_ref, z_ref, la_ref, lc_ref, om_ref, sums_ref, pairs_ref, s0_ref,
                 o_ref, sfin_ref, st_ref, *, reverse):
    _, _, starts = _hgrn_segment(pl.program_id(1), reverse)

    @pl.when(starts)
    def _():
        st_ref[...] = s0_ref[...]

    qr = q_ref[...]
    q = qr * jax.nn.sigmoid(qr) * HEAD_DIM ** -0.5
    z = z_ref[...]
    log_sig = jnp.minimum(z, 0.0) - jnp.log1p(jnp.exp(-jnp.abs(z)))
    a = la_ref[...]
    b = lc_ref[...] + log_sig
    log_f = jnp.maximum(a, b) + jnp.log1p(jnp.exp(-jnp.abs(a - b)))
    k = om_ref[...] * jax.nn.sigmoid(-z)
    v = v_ref[...].astype(BF16)
    hi = log_f.astype(BF16)
    lo = (log_f - hi.astype(F32)).astype(BF16)
    parts = jnp.concatenate([hi, lo], axis=1)

    def block_sum(i):
        s2 = jnp.dot(sums_ref[i], parts, preferred_element_type=F32)
        return s2[:, :HEAD_DIM] + s2[:, HEAD_DIM:]

    scores = pairs_ref[HG_LEVELS] * _dot_nt(q.astype(BF16), k.astype(BF16))
    for l in range(HG_LEVELS):
        e = jnp.exp(block_sum(l))
        scores = scores + pairs_ref[l] * _dot_nt((q * e).astype(BF16), (k * e).astype(BF16))
    o = jnp.dot(scores.astype(BF16), v, preferred_element_type=F32)

    state = st_ref[...]
    q_dec = (q * jnp.exp(block_sum(HG_LEVELS))).astype(BF16)
    o_ref[...] = o + jnp.dot(q_dec, state.astype(BF16), preferred_element_type=F32)

    k_dec = (k * jnp.exp(block_sum(HG_LEVELS + 1))).astype(BF16)
    total = _dot_tn(parts, jnp.ones((HG_T, HEAD_DIM), BF16))
    decay = jnp.exp(total[:HEAD_DIM] + total[HEAD_DIM:])
    new_state = state * decay + _dot_tn(k_dec, v)
    st_ref[...] = new_state
    sfin_ref[...] = new_state


def _hgrn_scan(z, la, lc, om, s0_all, reverse):
    sums, pairs = _hgrn_tables(reverse)
    gate_col = (3 if reverse else 2) * H_A

    def seg_of(g):
        return _hgrn_segment(g, reverse)[0]

    def seq_of(g):
        return _hgrn_segment(g, reverse)[1]

    row_block = lambda col0: pl.BlockSpec((HG_T, HEAD_DIM), lambda h, g: (seg_of(g), col0 + h))
    chan = pl.BlockSpec((1, HEAD_DIM), lambda h, g: (0, h))
    state_block = pl.BlockSpec((None, None, HEAD_DIM, HEAD_DIM), lambda h, g: (seq_of(g), h, 0, 0))
    return pl.pallas_call(
        functools.partial(_hgrn_kernel, reverse=reverse),
        grid=(H_A, N_SEG),
        in_specs=[row_block(0), row_block(H_A), row_block(gate_col), chan, chan, chan,
                  pl.BlockSpec(sums.shape, lambda h, g: (0, 0, 0)),
                  pl.BlockSpec(pairs.shape, lambda h, g: (0, 0, 0)),
                  state_block],
        out_specs=[pl.BlockSpec((HG_T, HEAD_DIM), lambda h, g: (seg_of(g), h)), state_block],
        out_shape=[jax.ShapeDtypeStruct((N_TOK, A_W), F32),
                   jax.ShapeDtypeStruct((N_SEQ, H_A, HEAD_DIM, HEAD_DIM), F32)],
        scratch_shapes=[pltpu.VMEM((HEAD_DIM, HEAD_DIM), F32)],
        compiler_params=_params(("arbitrary", "arbitrary")),
        name="hgrn_scan_bwd" if reverse else "hgrn_scan_fwd",
    )(z, z, z, la, lc, om, sums, pairs, s0_all)


def _hgrn_finish_kernel(of_ref, ob_ref, og_ref, gn_ref, o_ref):
    o = of_ref[...] + ob_ref[...]
    og = og_ref[...]
    for h in range(H_A):
        cols = slice(h * HEAD_DIM, (h + 1) * HEAD_DIM)
        oh = o[:, cols]
        gh = og[:, cols]
        y = oh * lax.rsqrt(jnp.mean(oh * oh, axis=-1, keepdims=True) + EPS) * gn_ref[...]
        o_ref[:, cols] = (y * (gh * jax.nn.sigmoid(gh))).astype(o_ref.dtype)


def _hgrn_finish(o_f, o_b, z, gn_g):
    tm = 512
    blk = pl.BlockSpec((tm, A_W), lambda i: (i, 0))
    return pl.pallas_call(
        _hgrn_finish_kernel,
        grid=(N_TOK // tm,),
        in_specs=[blk, blk,
                  pl.BlockSpec((tm, A_W), lambda i: (i, 4)),
                  pl.BlockSpec((1, HEAD_DIM), lambda i: (0, 0))],
        out_specs=blk,
        out_shape=jax.ShapeDtypeStruct((N_TOK, A_W), BF16),
        compiler_params=_params(("arbitrary",)),
        name="hgrn_finish",
    )(o_f, o_b, z, gn_g.reshape(1, HEAD_DIM))


def _hgrn2(z, lb_f, lb_b, gn_g, state_l):
    outs, finals = [], []
    for reverse, lb in ((False, lb_f), (True, lb_b)):
        la = jnp.log(jnp.maximum(lb, LB_FLOOR)).reshape(1, A_W)
        lc = jnp.log1p(-lb).reshape(1, A_W)
        om = (1.0 - lb).reshape(1, A_W)
        s0_all = jnp.concatenate(
            [jnp.zeros((BATCH, H_A, HEAD_DIM, HEAD_DIM), F32), state_l[:, int(reverse)]], axis=0)
        o, sfin = _hgrn_scan(z, la, lc, om, s0_all, reverse)
        outs.append(o)
        finals.append(sfin[:BATCH])
    return _hgrn_finish(outs[0], outs[1], z, gn_g), jnp.stack(finals, axis=1)


CONV_HALO = 16


def _conv_kernel(prev_ref, cur_ref, next_ref, dw_ref, dwb_ref, lng_ref, lnb_ref, pw_ref, pwb_ref,
                 o_ref, ext_ref, pwb16_ref):
    i = pl.program_id(0)

    @pl.when(i == 0)
    def _():
        pwb16_ref[...] = pw_ref[...].astype(BF16)

    def glu(x):
        return x[:, :B_W] * jax.nn.sigmoid(x[:, B_W:])

    lat = i - CTX_SEGS
    pos = lax.rem(lat, LAT_SEGS_PER_SEQ)
    has_prev = jnp.logical_and(i >= CTX_SEGS, pos != 0)
    has_next = jnp.logical_and(i >= CTX_SEGS, pos != LAT_SEGS_PER_SEQ - 1)
    ext_ref[0:CONV_HALO, :] = jnp.where(has_prev, glu(prev_ref[HG_T - CONV_HALO:, :]), 0.0)
    ext_ref[CONV_HALO:CONV_HALO + HG_T, :] = glu(cur_ref[...])
    ext_ref[CONV_HALO + HG_T:, :] = jnp.where(has_next, glu(next_ref[:CONV_HALO, :]), 0.0)

    first = CONV_HALO - CONV_W // 2
    acc = ext_ref[first:first + HG_T, :] * dw_ref[0:1, :]
    for j in range(1, CONV_W):
        acc = acc + ext_ref[first + j:first + j + HG_T, :] * dw_ref[j:j + 1, :]
    u = acc + dwb_ref[...]
    mu = jnp.mean(u, axis=-1, keepdims=True)
    d = u - mu
    var = jnp.mean(d * d, axis=-1, keepdims=True)
    y = d * lax.rsqrt(var + EPS) * lng_ref[...] + lnb_ref[...]
    y = y * jax.nn.sigmoid(y)
    o = jnp.dot(y.astype(BF16), pwb16_ref[...], preferred_element_type=F32) + pwb_ref[...]
    o_ref[...] = o.astype(o_ref.dtype)


def _conv_module(glu_in, lw):
    row = lambda d: pl.BlockSpec((1, B_W), lambda i: (0, 0))
    seg = lambda off: pl.BlockSpec((HG_T, 2 * B_W), lambda i: (jnp.clip(i + off, 0, N_SEG - 1), 0))
    vec = lambda a: a.reshape(1, B_W)
    return pl.pallas_call(
        _conv_kernel,
        grid=(N_SEG,),
        in_specs=[seg(-1), seg(0), seg(1),
                  pl.BlockSpec((CONV_W, B_W), lambda i: (0, 0)), row(0), row(0), row(0),
                  pl.BlockSpec((B_W, B_W), lambda i: (0, 0)), row(0)],
        out_specs=pl.BlockSpec((HG_T, B_W), lambda i: (i, 0)),
        out_shape=jax.ShapeDtypeStruct((N_TOK, B_W), BF16),
        scratch_shapes=[pltpu.VMEM((HG_T + 2 * CONV_HALO, B_W), F32),
                        pltpu.VMEM((B_W, B_W), BF16)],
        compiler_params=_params(("arbitrary",)),
        name="conv_module",
    )(glu_in, glu_in, glu_in, lw['dw_w'], vec(lw['dw_b']), vec(lw['ln_g']), vec(lw['ln_b']),
      lw['pw_w'], vec(lw['pw_b']))


COL_GLU = 5 * H_A
COL_QC = COL_GLU + 2 * B_W // HEAD_DIM
COL_KC = COL_QC + H_C
COL_VC = COL_KC + H_C

NA_QROWS = 4
NA_QB = NA_QROWS * GRID_W
NA_KROWS = NA_QROWS + KH_MAX
NA_KEYS = NA_KROWS * GRID_W
NA_BLOCKS = DEC_SEQ // NA_QB
GRID_ROWS = DEC_SEQ // GRID_W


def _na_key_row0(first_query_row):
    clip = np.clip if isinstance(first_query_row, (int, np.integer)) else jnp.clip
    return clip(first_query_row - KH_MAX // 2, 0, GRID_ROWS - NA_KROWS)


def _na_bias_tables(rpb):
    kh = min(KH_MAX, GRID_ROWS)
    tables = []
    for first_row in (0, NA_QROWS, 2 * NA_QROWS, GRID_ROWS - NA_QROWS):
        key_row0 = int(_na_key_row0(first_row))
        r = first_row + np.arange(NA_QROWS)[:, None, None, None]
        c = np.arange(GRID_W)[None, :, None, None]
        kr = key_row0 + np.arange(NA_KROWS)[None, None, :, None]
        kc = np.arange(GRID_W)[None, None, None, :]
        row_start = np.clip(r - kh // 2, 0, GRID_ROWS - kh)
        col_start = np.clip(c - KW // 2, 0, GRID_W - KW)
        valid = (kr >= row_start) & (kr < row_start + kh) & (kc >= col_start) & (kc < col_start + KW)
        dr = np.clip(kr - r + KH_MAX - 1, 0, 2 * KH_MAX - 2)
        dc = np.clip(kc - c, -(KW - 1), KW - 1) + KW - 1
        shape = (NA_QROWS, GRID_W, NA_KROWS, GRID_W)
        dr, dc, valid = (np.broadcast_to(a, shape).reshape(NA_QB, NA_KEYS) for a in (dr, dc, valid))
        tables.append(jnp.where(valid[None], rpb.astype(F32)[:, dr, dc], NEG_BIG))
    return jnp.stack(tables, axis=0)


def _softmax_pv(scores, values):
    m = functools.reduce(jnp.maximum, [jnp.max(s, axis=-1, keepdims=True) for s in scores])
    ps = [jnp.exp(s - m) for s in scores]
    denom = functools.reduce(jnp.add, [jnp.sum(p, axis=-1, keepdims=True) for p in ps])
    o = functools.reduce(jnp.add, [jnp.dot(p.astype(BF16), v, preferred_element_type=F32)
                                   for p, v in zip(ps, values)])
    return o / denom


def _na_kernel(q_ref, k_ref, v_ref, kc_ref, vc_ref, bias_ref, o_ref):
    key_row0 = _na_key_row0(pl.program_id(2) * NA_QROWS)
    keys = pl.ds(pl.multiple_of(key_row0 * GRID_W, NA_QB), NA_KEYS)
    scale = HEAD_DIM ** -0.5
    q = q_ref[...].astype(BF16)
    s_loc = _dot_nt(q, k_ref[keys, :].astype(BF16)) * scale + bias_ref[...]
    s_ctx = _dot_nt(q, kc_ref[...].astype(BF16)) * scale
    o = _softmax_pv([s_loc, s_ctx], [v_ref[keys, :].astype(BF16), vc_ref[...].astype(BF16)])
    o_ref[...] = o.astype(o_ref.dtype)


def _na_latent(z, cache_k, cache_v, rpb):
    bias = _na_bias_tables(rpb)
    q_blocks0 = N_CTX // NA_QB
    seq_blocks0 = N_CTX // DEC_SEQ

    def block_type(qb):
        return jnp.where(qb < 2, qb, jnp.where(qb == NA_BLOCKS - 1, 3, 2))

    seq_spec = lambda col0: pl.BlockSpec((DEC_SEQ, HEAD_DIM), lambda b, h, qb: (seq_blocks0 + b, col0 + h))
    ctx_spec = pl.BlockSpec((None, PAST_LEN, HEAD_DIM), lambda b, h, qb: (b, 0, h))
    return pl.pallas_call(
        _na_kernel,
        grid=(DEC_BATCH, H_C, NA_BLOCKS),
        in_specs=[pl.BlockSpec((NA_QB, HEAD_DIM), lambda b, h, qb: (q_blocks0 + b * NA_BLOCKS + qb, COL_QC + h)),
                  seq_spec(COL_KC), seq_spec(COL_VC), ctx_spec, ctx_spec,
                  pl.BlockSpec((None, None, NA_QB, NA_KEYS), lambda b, h, qb: (block_type(qb), h, 0, 0))],
        out_specs=pl.BlockSpec((NA_QB, HEAD_DIM), lambda b, h, qb: (b * NA_BLOCKS + qb, h)),
        out_shape=jax.ShapeDtypeStruct((N_LAT, C_W), BF16),
        compiler_params=_params(("arbitrary", "arbitrary", "arbitrary")),
        name="na_attention",
    )(z, z, z, cache_k.reshape(DEC_BATCH, PAST_LEN, C_W), cache_v.reshape(DEC_BATCH, PAST_LEN, C_W), bias)


def _ctx_attn_kernel(q_ref, k_ref, v_ref, o_ref):
    scale = HEAD_DIM ** -0.5
    s = _dot_nt(q_ref[...].astype(BF16), k_ref[...].astype(BF16)) * scale
    o_ref[...] = _softmax_pv([s], [v_ref[...].astype(BF16)]).astype(o_ref.dtype)


def _context_attention(z):
    spec = lambda col0: pl.BlockSpec((SEQ, HEAD_DIM), lambda b, h: (b, col0 + h))
    return pl.pallas_call(
        _ctx_attn_kernel,
        grid=(BATCH, H_C),
        in_specs=[spec(COL_QC), spec(COL_KC), spec(COL_VC)],
        out_specs=pl.BlockSpec((SEQ, HEAD_DIM), lambda b, h: (b, h)),
        out_shape=jax.ShapeDtypeStruct((N_CTX, C_W), BF16),
        compiler_params=_params(("arbitrary", "arbitrary")),
        name="context_attention",
    )(z, z, z)


def _mixers(z, lw, cache_k, cache_v, state_l):
    o_a, state = _hgrn2(z, lw['lb_f'], lw['lb_b'], lw['gn_g'], state_l)
    o_b = _conv_module(z[:, 5 * A_W:5 * A_W + 2 * B_W], lw)
    o_c = jnp.concatenate([_context_attention(z), _na_latent(z, cache_k, cache_v, lw['rpb'])], axis=0)
    kv = z[:N_CTX, COL_KC * HEAD_DIM:].reshape(BATCH, SEQ, 2, H_C, HEAD_DIM)
    return jnp.concatenate([o_a, o_b, o_c], axis=-1), kv[:, :, 0], kv[:, :, 1], state


def kernel(x_prompt, x_sample, cache_na_k, cache_na_v, state_hgrn, c, c_ctx, w_mod, b_mod,
           norm1_g, norm2_g, w_in, hgrn_lb, hgrn_gn_g, conv_dw_w, conv_dw_b, conv_ln_g,
           conv_ln_b, conv_pw_w, conv_pw_b, na_rpb, w_out, w_router, b_router, w_up, b_up,
           w_down, b_down, final_g):
    lb_soft = jax.nn.softmax(hgrn_lb.astype(F32), axis=1)
    lower_bounds = jnp.cumsum(lb_soft, axis=1) - lb_soft[:, :1]

    cond = jnp.zeros((8, D_MODEL), F32).at[0].set(c_ctx).at[1:1 + DEC_BATCH].set(c)
    mods = _modulation_all(cond, w_mod, b_mod)
    mods = mods[:, :N_SEQ_ROWS].reshape(DEPTH, N_SEQ_ROWS, 6, 1, D_MODEL).transpose(0, 2, 1, 3, 4)

    x = jnp.concatenate([x_prompt.reshape(N_CTX, D_MODEL), x_sample.reshape(N_LAT, D_MODEL)], axis=0)
    ks_new, vs_new, ss_new = [], [], []
    for l in range(DEPTH):
        sh1, sc1, g1, sh2, sc2, g2 = (mods[l, i] for i in range(6))
        lw = dict(lb_f=lower_bounds[0, l], lb_b=lower_bounds[1, l], gn_g=hgrn_gn_g[l],
                  dw_w=conv_dw_w[l], dw_b=conv_dw_b[l], ln_g=conv_ln_g[l], ln_b=conv_ln_b[l],
                  pw_w=conv_pw_w[l], pw_b=conv_pw_b[l], rpb=na_rpb[l])
        h = _norm_mod(x, norm1_g[l], sc1, sh1, BF16)
        z = _proj(h, w_in[l])
        mix, k_l, v_l, s_l = _mixers(z, lw, cache_na_k[:, l], cache_na_v[:, l], state_hgrn[:, l])
        ks_new.append(k_l)
        vs_new.append(v_l)
        ss_new.append(s_l)
        x = _proj_residual(mix, w_out[l], x, g1)

        h2, logits = _norm_mod_router(x, norm2_g[l], sc2, sh2, w_router[l], b_router[l])
        gates, dest, src_tok, blk_e, nsub, last_used = _route(logits[:, :N_EXPERTS])
        xs = _dispatch(h2, src_tok, nsub)
        y = _moe_experts(xs, blk_e, nsub, last_used, w_up[l], b_up[l], w_down[l], b_down[l])
        x = _combine(x, y, dest, gates, g2)

    y_all = _final_norm(x, final_g)
    y_prompt = y_all[:N_CTX].reshape(BATCH, SEQ, D_MODEL)
    y_sample = y_all[N_CTX:].reshape(DEC_BATCH, DEC_SEQ, D_MODEL)
    return (y_prompt, y_sample, jnp.stack(ks_new, axis=1), jnp.stack(vs_new, axis=1),
            jnp.stack(ss_new, axis=1))
```

```python
import functools

import jax
import jax.numpy as jnp
import numpy as np
from jax import lax
from jax.experimental import pallas as pl
from jax.experimental.pallas import tpu as pltpu

D_MODEL = 2048
BATCH = 32
SEQ = 256
DEPTH = 4
DEC_BATCH = 4
DEC_SEQ = 2048
PAST_LEN = 256
GRID_W = 64
HEAD_DIM = 128
A_W = 3 * D_MODEL // 8
H_A = A_W // HEAD_DIM
B_W = D_MODEL // 4
C_W = D_MODEL - A_W - B_W
H_C = C_W // HEAD_DIM
MIX_W = A_W + B_W + C_W
IN_COLS = 5 * A_W + 2 * B_W + 3 * C_W
CONV_W = 31
KH_MAX = 8
KW = 16
N_EXPERTS = 32
TOP_K = 4
D_FF = D_MODEL
SWIGLU_ALPHA = 1.702
SWIGLU_LIMIT = 7.0
EPS = 1e-6
NEG_BIG = -1e30
LB_FLOOR = 1e-30
F32 = jnp.float32
BF16 = jnp.bfloat16

N_CTX = BATCH * SEQ
N_LAT = DEC_BATCH * DEC_SEQ
N_TOK = N_CTX + N_LAT
N_SEQ_ROWS = 1 + DEC_BATCH

V7X_VMEM_LIMIT_BYTES = 56 * 1024 * 1024
V7X_VMEM_LIMIT_MOE_BYTES = 60 * 1024 * 1024

TM = 1024
TN = 1024
MOE_R = 1024
MOE_SUB = 256
MOE_TF = 512
MOE_L = N_TOK * TOP_K + N_EXPERTS * MOE_R
MOE_S = MOE_L // MOE_R
COMBINE_TOK = 256


def _seq_row(i, tm):
    n_ctx_blocks = N_CTX // tm
    per_seq = DEC_SEQ // tm
    return jnp.where(i < n_ctx_blocks, 0, 1 + (i - n_ctx_blocks) // per_seq)


def _params(sem, vmem_limit_bytes=V7X_VMEM_LIMIT_BYTES):
    return pltpu.CompilerParams(dimension_semantics=sem, vmem_limit_bytes=vmem_limit_bytes)


def _mod_kernel(c_ref, w_ref, b_ref, o_ref):
    s = c_ref[...]
    s = s * jax.nn.sigmoid(s)
    o_ref[...] = jnp.dot(s.astype(BF16), w_ref[...].astype(BF16),
                         preferred_element_type=F32) + b_ref[...]


def _modulation_all(cond, w_mod, b_mod):
    tn = 1024
    n = 6 * D_MODEL
    return pl.pallas_call(
        _mod_kernel,
        grid=(DEPTH, n // tn),
        in_specs=[
            pl.BlockSpec((8, D_MODEL), lambda l, j: (0, 0)),
            pl.BlockSpec((None, D_MODEL, tn), lambda l, j: (l, 0, j)),
            pl.BlockSpec((None, 1, tn), lambda l, j: (l, 0, j)),
        ],
        out_specs=pl.BlockSpec((None, 8, tn), lambda l, j: (l, 0, j)),
        out_shape=jax.ShapeDtypeStruct((DEPTH, 8, n), F32),
        compiler_params=_params(("arbitrary", "arbitrary")),
        name="modulation",
    )(cond, w_mod, b_mod.reshape(DEPTH, 1, n))


def _rms(x, g):
    return x * lax.rsqrt(jnp.mean(x * x, axis=-1, keepdims=True) + EPS) * g


def _norm_mod_kernel(x_ref, g_ref, sc_ref, sh_ref, o_ref):
    y = _rms(x_ref[...], g_ref[...])
    o_ref[...] = (y * (1.0 + sc_ref[...]) + sh_ref[...]).astype(o_ref.dtype)


def _norm_mod(x, g, sc, sh, dtype):
    tm = 512
    return pl.pallas_call(
        _norm_mod_kernel,
        grid=(N_TOK // tm,),
        in_specs=[
            pl.BlockSpec((tm, D_MODEL), lambda i: (i, 0)),
            pl.BlockSpec((1, D_MODEL), lambda i: (0, 0)),
            pl.BlockSpec((None, 1, D_MODEL), lambda i: (_seq_row(i, tm), 0, 0)),
            pl.BlockSpec((None, 1, D_MODEL), lambda i: (_seq_row(i, tm), 0, 0)),
        ],
        out_specs=pl.BlockSpec((tm, D_MODEL), lambda i: (i, 0)),
        out_shape=jax.ShapeDtypeStruct((N_TOK, D_MODEL), dtype),
        compiler_params=_params(("arbitrary",)),
        name="norm_mod",
    )(x, g.reshape(1, D_MODEL), sc, sh)


def _norm_mod_router_kernel(x_ref, g_ref, sc_ref, sh_ref, wr_ref, br_ref, o_ref, lg_ref):
    y = _rms(x_ref[...], g_ref[...])
    h = y * (1.0 + sc_ref[...]) + sh_ref[...]
    o_ref[...] = h.astype(o_ref.dtype)
    lg_ref[...] = jnp.dot(h, wr_ref[...], preferred_element_type=F32,
                          precision=lax.Precision.HIGHEST) + br_ref[...]


def _norm_mod_router(x, g, sc, sh, w_router, b_router):
    tm = 512
    wr = jnp.zeros((D_MODEL, 128), F32).at[:, :N_EXPERTS].set(w_router)
    br = jnp.zeros((1, 128), F32).at[0, :N_EXPERTS].set(b_router)
    return pl.pallas_call(
        _norm_mod_router_kernel,
        grid=(N_TOK // tm,),
        in_specs=[
            pl.BlockSpec((tm, D_MODEL), lambda i: (i, 0)),
            pl.BlockSpec((1, D_MODEL), lambda i: (0, 0)),
            pl.BlockSpec((None, 1, D_MODEL), lambda i: (_seq_row(i, tm), 0, 0)),
            pl.BlockSpec((None, 1, D_MODEL), lambda i: (_seq_row(i, tm), 0, 0)),
            pl.BlockSpec((D_MODEL, 128), lambda i: (0, 0)),
            pl.BlockSpec((1, 128), lambda i: (0, 0)),
        ],
        out_specs=[pl.BlockSpec((tm, D_MODEL), lambda i: (i, 0)),
                   pl.BlockSpec((tm, 128), lambda i: (i, 0))],
        out_shape=[jax.ShapeDtypeStruct((N_TOK, D_MODEL), F32),
                   jax.ShapeDtypeStruct((N_TOK, 128), F32)],
        compiler_params=_params(("arbitrary",)),
        name="norm_mod_router",
    )(x, g.reshape(1, D_MODEL), sc, sh, wr, br)


def _final_norm_kernel(x_ref, g_ref, o_ref):
    o_ref[...] = _rms(x_ref[...], g_ref[...])


def _final_norm(x, g):
    tm = 512
    return pl.pallas_call(
        _final_norm_kernel,
        grid=(N_TOK // tm,),
        in_specs=[pl.BlockSpec((tm, D_MODEL), lambda i: (i, 0)),
                  pl.BlockSpec((1, D_MODEL), lambda i: (0, 0))],
        out_specs=pl.BlockSpec((tm, D_MODEL), lambda i: (i, 0)),
        out_shape=jax.ShapeDtypeStruct((N_TOK, D_MODEL), F32),
        compiler_params=_params(("arbitrary",)),
        name="final_norm",
    )(x, g.reshape(1, D_MODEL))


def _proj_kernel(a_ref, w_ref, o_ref, wb_ref):
    @pl.when(pl.program_id(1) == 0)
    def _():
        wb_ref[...] = w_ref[...].astype(BF16)

    o_ref[...] = jnp.dot(a_ref[...], wb_ref[...], preferred_element_type=F32)


def _proj(a, w):
    k, n = w.shape
    return pl.pallas_call(
        _proj_kernel,
        grid=(n // TN, N_TOK // TM),
        in_specs=[pl.BlockSpec((TM, k), lambda j, i: (i, 0)),
                  pl.BlockSpec((k, TN), lambda j, i: (0, j))],
        out_specs=pl.BlockSpec((TM, TN), lambda j, i: (i, j)),
        out_shape=jax.ShapeDtypeStruct((N_TOK, n), F32),
        scratch_shapes=[pltpu.VMEM((k, TN), BF16)],
        compiler_params=_params(("arbitrary", "arbitrary")),
        name="proj",
    )(a, w)


def _proj_residual_kernel(a_ref, w_ref, x_ref, gate_ref, o_ref, wb_ref):
    @pl.when(pl.program_id(1) == 0)
    def _():
        wb_ref[...] = w_ref[...].astype(BF16)

    y = jnp.dot(a_ref[...], wb_ref[...], preferred_element_type=F32)
    o_ref[...] = x_ref[...] + gate_ref[...] * y


def _proj_residual(a, w, x, gate):
    k, n = w.shape
    return pl.pallas_call(
        _proj_residual_kernel,
        grid=(n // TN, N_TOK // TM),
        in_specs=[pl.BlockSpec((TM, k), lambda j, i: (i, 0)),
                  pl.BlockSpec((k, TN), lambda j, i: (0, j)),
                  pl.BlockSpec((TM, TN), lambda j, i: (i, j)),
                  pl.BlockSpec((None, 1, TN), lambda j, i: (_seq_row(i, TM), 0, j))],
        out_specs=pl.BlockSpec((TM, TN), lambda j, i: (i, j)),
        out_shape=jax.ShapeDtypeStruct((N_TOK, n), F32),
        scratch_shapes=[pltpu.VMEM((k, TN), BF16)],
        compiler_params=_params(("arbitrary", "arbitrary")),
        name="proj_residual",
    )(a, w, x, gate)


def _swiglu(g, u):
    g = jnp.minimum(g, SWIGLU_LIMIT)
    u = jnp.clip(u, -SWIGLU_LIMIT, SWIGLU_LIMIT)
    return g * jax.nn.sigmoid(SWIGLU_ALPHA * g) * (u + 1.0)


def _moe_kernel(blk_e_ref, nsub_ref, last_ref, xs_ref, wg_ref, wu_ref, wd_ref, bg_ref, bu_ref, bd_ref,
                o_ref, wgb_ref, wub_ref, wdb_ref):
    s = pl.program_id(0)
    f = pl.program_id(1)
    n = nsub_ref[s]

    def contribution(m):
        rows = pl.ds(pl.multiple_of(m * MOE_SUB, MOE_SUB), MOE_SUB)
        x = xs_ref[rows, :]
        g = jnp.dot(x, wgb_ref[...], preferred_element_type=F32) + bg_ref[...]
        u = jnp.dot(x, wub_ref[...], preferred_element_type=F32) + bu_ref[...]
        a = _swiglu(g, u).astype(BF16)
        return rows, jnp.dot(a, wdb_ref[...], preferred_element_type=F32)

    @pl.when(n > 0)
    def _():
        wgb_ref[...] = wg_ref[...].astype(BF16)
        wub_ref[...] = wu_ref[...].astype(BF16)
        wdb_ref[...] = wd_ref[...].astype(BF16)

        @pl.when(f == 0)
        def _():
            def body(m, carry):
                rows, y = contribution(m)
                o_ref[rows, :] = y + bd_ref[...]
                return carry
            lax.fori_loop(0, n, body, 0)

        @pl.when(f > 0)
        def _():
            def body(m, carry):
                rows, y = contribution(m)
                o_ref[rows, :] += y
                return carry
            lax.fori_loop(0, n, body, 0)


def _moe_experts(xs, blk_e, nsub, last_used, w_up, b_up, w_down, b_down):
    nf = D_FF // MOE_TF

    def blk(s, last):
        return jnp.minimum(s, last[0])

    grid_spec = pltpu.PrefetchScalarGridSpec(
        num_scalar_prefetch=3,
        grid=(MOE_S, nf),
        in_specs=[
            pl.BlockSpec((MOE_R, D_MODEL), lambda s, f, e, n, last: (blk(s, last), 0)),
            pl.BlockSpec((None, D_MODEL, MOE_TF),
                         lambda s, f, e, n, last: (e[s], 0, jnp.where(n[s] > 0, f, nf - 1))),
            pl.BlockSpec((None, D_MODEL, MOE_TF),
                         lambda s, f, e, n, last: (e[s], 0, nf + jnp.where(n[s] > 0, f, nf - 1))),
            pl.BlockSpec((None, MOE_TF, D_MODEL),
                         lambda s, f, e, n, last: (e[s], jnp.where(n[s] > 0, f, nf - 1), 0)),
            pl.BlockSpec((None, 1, MOE_TF),
                         lambda s, f, e, n, last: (e[s], 0, jnp.where(n[s] > 0, f, nf - 1))),
            pl.BlockSpec((None, 1, MOE_TF),
                         lambda s, f, e, n, last: (e[s], 0, nf + jnp.where(n[s] > 0, f, nf - 1))),
            pl.BlockSpec((None, 1, D_MODEL), lambda s, f, e, n, last: (e[s], 0, 0)),
        ],
        out_specs=pl.BlockSpec((MOE_R, D_MODEL), lambda s, f, e, n, last: (blk(s, last), 0)),
        scratch_shapes=[pltpu.VMEM((D_MODEL, MOE_TF), BF16),
                        pltpu.VMEM((D_MODEL, MOE_TF), BF16),
                        pltpu.VMEM((MOE_TF, D_MODEL), BF16)],
    )
    return pl.pallas_call(
        _moe_kernel,
        grid_spec=grid_spec,
        out_shape=jax.ShapeDtypeStruct((MOE_L, D_MODEL), F32),
        compiler_params=_params(("arbitrary", "arbitrary"), V7X_VMEM_LIMIT_MOE_BYTES),
        name="moe_experts",
    )(blk_e, nsub, last_used, xs, w_up, w_up, w_down,
      b_up.reshape(N_EXPERTS, 1, 2 * D_FF), b_up.reshape(N_EXPERTS, 1, 2 * D_FF),
      b_down.reshape(N_EXPERTS, 1, D_MODEL))


def _route(logits):
    top_val, top_idx = lax.top_k(logits, TOP_K)
    gates = jax.nn.softmax(top_val, axis=-1)
    flat_e = top_idx.reshape(-1)
    onehot = (flat_e[:, None] == jnp.arange(N_EXPERTS)[None, :]).astype(jnp.int32)
    rank = jnp.sum((jnp.cumsum(onehot, axis=0) - onehot) * onehot, axis=1)
    counts = jnp.sum(onehot, axis=0)
    padded = (counts + MOE_R - 1) // MOE_R * MOE_R
    pad_ends = jnp.cumsum(padded)
    pad_starts = pad_ends - padded
    dest = (pad_starts[flat_e] + rank).astype(jnp.int32)
    blk_start = jnp.arange(MOE_S, dtype=jnp.int32) * MOE_R
    blk_e = jnp.minimum(jnp.searchsorted(pad_ends, blk_start, side='right'),
                        N_EXPERTS - 1).astype(jnp.int32)
    used = blk_start < pad_ends[-1]
    in_blk = jnp.clip(pad_starts[blk_e] + counts[blk_e] - blk_start, 0, MOE_R)
    nsub = jnp.where(used, (in_blk + MOE_SUB - 1) // MOE_SUB, 0).astype(jnp.int32)
    last_used = (pad_ends[-1] // MOE_R - 1).astype(jnp.int32).reshape(1)
    blk_e = jnp.where(used, blk_e, blk_e[last_used[0]])
    tok = jnp.arange(N_TOK * TOP_K, dtype=jnp.int32) // TOP_K
    src_tok = jnp.zeros((MOE_L,), jnp.int32).at[dest].set(tok)
    return gates, dest, src_tok, blk_e, nsub, last_used


def _row_copy(src_hbm, src_row, dst, dst_row, sem):
    return pltpu.make_async_copy(src_hbm.at[pl.ds(src_row, 1)], dst.at[pl.ds(dst_row, 1)], sem)


ROW_ISSUE_UNROLL = 8


def _dispatch_kernel(nsub_ref, src_ref, src_next_ref, h_hbm, o_ref, buf_ref, sems):
    i = pl.program_id(0)
    n_steps = pl.num_programs(0)
    slot = i % 2

    def issue(step, s_ref, to_slot):
        def body(c, carry):
            for u in range(ROW_ISSUE_UNROLL):
                r = c * ROW_ISSUE_UNROLL + u
                _row_copy(h_hbm, s_ref[0, r], buf_ref.at[to_slot], r, sems.at[to_slot]).start()
            return carry
        lax.fori_loop(0, nsub_ref[step] * (MOE_SUB // ROW_ISSUE_UNROLL), body, 0)

    @pl.when(i == 0)
    def _():
        issue(0, src_ref, 0)

    @pl.when(i + 1 < n_steps)
    def _():
        issue(i + 1, src_next_ref, 1 - slot)

    for g in range(MOE_R // MOE_SUB):
        @pl.when(g < nsub_ref[i])
        def _():
            pltpu.make_async_copy(h_hbm.at[pl.ds(0, MOE_SUB)], buf_ref.at[slot, pl.ds(0, MOE_SUB)],
                                  sems.at[slot]).wait()

    for g in range(MOE_R // MOE_SUB):
        @pl.when(g < nsub_ref[i])
        def _():
            rows = pl.ds(g * MOE_SUB, MOE_SUB)
            o_ref[rows, :] = buf_ref[slot, rows, :].astype(o_ref.dtype)


def _dispatch(h, src_tok, nsub):
    grid_spec = pltpu.PrefetchScalarGridSpec(
        num_scalar_prefetch=1,
        grid=(MOE_S,),
        in_specs=[
            pl.BlockSpec((None, 1, MOE_R), lambda i, n: (i, 0, 0), memory_space=pltpu.SMEM),
            pl.BlockSpec((None, 1, MOE_R), lambda i, n: (jnp.minimum(i + 1, MOE_S - 1), 0, 0),
                         memory_space=pltpu.SMEM),
            pl.BlockSpec(memory_space=pl.ANY),
        ],
        out_specs=pl.BlockSpec((MOE_R, D_MODEL), lambda i, n: (i, 0)),
        scratch_shapes=[pltpu.VMEM((2, MOE_R, D_MODEL), F32),
                        pltpu.SemaphoreType.DMA((2,))],
    )
    src = src_tok.reshape(MOE_S, 1, MOE_R)
    return pl.pallas_call(
        _dispatch_kernel,
        grid_spec=grid_spec,
        out_shape=jax.ShapeDtypeStruct((MOE_L, D_MODEL), BF16),
        compiler_params=_params(("arbitrary",)),
        name="moe_dispatch",
    )(nsub, src, src, h)


def _combine_kernel(dest_ref, dest_next_ref, x_ref, y_hbm, w_ref, gate_ref, o_ref, buf_ref, sems):
    i = pl.program_id(0)
    n_steps = pl.num_programs(0)
    slot = i % 2

    def issue(d_ref, to_slot):
        for k in range(TOP_K):
            def body(t, carry):
                _row_copy(y_hbm, d_ref[0, k * COMBINE_TOK + t], buf_ref.at[to_slot, k], t,
                          sems.at[to_slot]).start()
                return carry
            lax.fori_loop(0, COMBINE_TOK, body, 0, unroll=8)

    @pl.when(i == 0)
    def _():
        issue(dest_ref, 0)

    @pl.when(i + 1 < n_steps)
    def _():
        issue(dest_next_ref, 1 - slot)

    for k in range(TOP_K):
        pltpu.make_async_copy(y_hbm.at[pl.ds(0, COMBINE_TOK)], buf_ref.at[slot, k], sems.at[slot]).wait()

    w = w_ref[...]
    acc = w[:, 0:1] * buf_ref[slot, 0]
    for k in range(1, TOP_K):
        acc = acc + w[:, k:k + 1] * buf_ref[slot, k]
    o_ref[...] = x_ref[...] + gate_ref[...] * acc


def _combine(x, y, dest, gates, gate2):
    tm = COMBINE_TOK
    n_steps = N_TOK // tm
    gp = jnp.zeros((N_TOK, 128), F32).at[:, :TOP_K].set(gates)
    dest_km = dest.reshape(n_steps, tm, TOP_K).transpose(0, 2, 1).reshape(n_steps, 1, TOP_K * tm)
    return pl.pallas_call(
        _combine_kernel,
        grid=(n_steps,),
        in_specs=[pl.BlockSpec((None, 1, TOP_K * tm), lambda i: (i, 0, 0), memory_space=pltpu.SMEM),
                  pl.BlockSpec((None, 1, TOP_K * tm), lambda i: (jnp.minimum(i + 1, n_steps - 1), 0, 0),
                               memory_space=pltpu.SMEM),
                  pl.BlockSpec((tm, D_MODEL), lambda i: (i, 0)),
                  pl.BlockSpec(memory_space=pl.ANY),
                  pl.BlockSpec((tm, 128), lambda i: (i, 0)),
                  pl.BlockSpec((None, 1, D_MODEL), lambda i: (_seq_row(i, tm), 0, 0))],
        out_specs=pl.BlockSpec((tm, D_MODEL), lambda i: (i, 0)),
        out_shape=jax.ShapeDtypeStruct((N_TOK, D_MODEL), F32),
        scratch_shapes=[pltpu.VMEM((2, TOP_K, tm, D_MODEL), F32),
                        pltpu.SemaphoreType.DMA((2,))],
        compiler_params=_params(("arbitrary",)),
        name="moe_combine",
    )(dest_km, dest_km, x, y, gp, gate2)


HG_T = 256
HG_LEVELS = 8
N_SEG = N_TOK // HG_T
CTX_SEGS = N_CTX // HG_T
LAT_SEGS_PER_SEQ = DEC_SEQ // HG_T
N_SEQ = BATCH + DEC_BATCH


def _hgrn_tables(reverse):
    t = np.arange(HG_T)
    r, j = t[:, None], t[None, :]
    sums = np.zeros((HG_LEVELS + 2, HG_T, HG_T), np.float32)
    pairs = np.zeros((HG_LEVELS + 1, HG_T, HG_T), np.float32)
    for l in range(HG_LEVELS):
        same_block = (r >> l) == (j >> l)
        later_half = ((r >> l) & 1) == 1
        sums[l] = same_block & np.where(later_half, j <= r, j > r)
        pairs[l] = ((r >> (l + 1)) == (j >> (l + 1))) & later_half & (((j >> l) & 1) == 0)
    sums[HG_LEVELS] = j <= r
    sums[HG_LEVELS + 1] = j > r
    pairs[HG_LEVELS] = r == j
    if reverse:
        sums = sums[:, ::-1, ::-1]
        pairs = pairs[:, ::-1, ::-1]
    return jnp.asarray(sums, BF16), jnp.asarray(pairs, F32)


def _dot_nt(x, y):
    return lax.dot_general(x, y, (((1,), (1,)), ((), ())), preferred_element_type=F32)


def _dot_tn(x, y):
    return lax.dot_general(x, y, (((0,), (0,)), ((), ())), preferred_element_type=F32)


def _hgrn_segment(g, reverse):
    seg = (N_SEG - 1 - g) if reverse else g
    lat = seg - CTX_SEGS
    first = (LAT_SEGS_PER_SEQ - 1) if reverse else 0
    starts = jnp.logical_or(seg < CTX_SEGS, lax.rem(lat, LAT_SEGS_PER_SEQ) == first)
    seq = jnp.where(seg < CTX_SEGS, seg, CTX_SEGS + lat // LAT_SEGS_PER_SEQ)
    return seg, seq, starts


HG_HEADS = 3


def _hgrn_head(qr, vr, z, la, lc, om, sums_ref, pairs_ref, state):
    q = qr * jax.nn.sigmoid(qr) * HEAD_DIM ** -0.5
    log_sig = jnp.minimum(z, 0.0) - jnp.log1p(jnp.exp(-jnp.abs(z)))
    b = lc + log_sig
    log_f = jnp.maximum(la, b) + jnp.log1p(jnp.exp(-jnp.abs(la - b)))
    k = om * jax.nn.sigmoid(-z)
    v = vr.astype(BF16)
    hi = log_f.astype(BF16)
    lo = (log_f - hi.astype(F32)).astype(BF16)
    parts = jnp.concatenate([hi, lo], axis=1)

    def block_sum(i):
        s2 = jnp.dot(sums_ref[i], parts, preferred_element_type=F32)
        return s2[:, :HEAD_DIM] + s2[:, HEAD_DIM:]

    scores = pairs_ref[HG_LEVELS] * _dot_nt(q.astype(BF16), k.astype(BF16))
    for l in range(HG_LEVELS):
        e = jnp.exp(block_sum(l))
        scores = scores + pairs_ref[l] * _dot_nt((q * e).astype(BF16), (k * e).astype(BF16))
    o = jnp.dot(scores.astype(BF16), v, preferred_element_type=F32)

    q_dec = (q * jnp.exp(block_sum(HG_LEVELS))).astype(BF16)
    o = o + jnp.dot(q_dec, state.astype(BF16), preferred_element_type=F32)

    k_dec = (k * jnp.exp(block_sum(HG_LEVELS + 1))).astype(BF16)
    total = _dot_tn(parts, jnp.ones((HG_T, HEAD_DIM), BF16))
    decay = jnp.exp(total[:HEAD_DIM] + total[HEAD_DIM:])
    return o, state * decay + _dot_tn(k_dec, v)


def _hgrn_kernel(q_ref, v_ref, z_ref, la_ref, lc_ref, om_ref, sums_ref, pairs_ref, s0_ref,
                 o_ref, sfin_ref, st_ref, *, reverse):
    _, _, starts = _hgrn_segment(pl.program_id(1), reverse)

    @pl.when(starts)
    def _():
        st_ref[...] = s0_ref[...]

    for h in range(HG_HEADS):
        cols = slice(h * HEAD_DIM, (h + 1) * HEAD_DIM)
        o, new_state = _hgrn_head(q_ref[:, cols], v_ref[:, cols], z_ref[:, cols], la_ref[:, cols],
                                  lc_ref[:, cols], om_ref[:, cols], sums_ref, pairs_ref, st_ref[h])
        o_ref[:, cols] = o
        st_ref[h] = new_state
        sfin_ref[h] = new_state


def _hgrn_scan(z, la, lc, om, s0_all, reverse):
    sums, pairs = _hgrn_tables(reverse)
    width = HG_HEADS * HEAD_DIM
    groups = H_A // HG_HEADS
    gate_col = (3 if reverse else 2) * groups

    def seg_of(g):
        return _hgrn_segment(g, reverse)[0]

    def seq_of(g):
        return _hgrn_segment(g, reverse)[1]

    row_block = lambda col0: pl.BlockSpec((HG_T, width), lambda h, g: (seg_of(g), col0 + h))
    chan = pl.BlockSpec((1, width), lambda h, g: (0, h))
    state_block = pl.BlockSpec((None, HG_HEADS, HEAD_DIM, HEAD_DIM), lambda h, g: (seq_of(g), h, 0, 0))
    return pl.pallas_call(
        functools.partial(_hgrn_kernel, reverse=reverse),
        grid=(groups, N_SEG),
        in_specs=[row_block(0), row_block(groups), row_block(gate_col), chan, chan, chan,
                  pl.BlockSpec(sums.shape, lambda h, g: (0, 0, 0)),
                  pl.BlockSpec(pairs.shape, lambda h, g: (0, 0, 0)),
                  state_block],
        out_specs=[pl.BlockSpec((HG_T, width), lambda h, g: (seg_of(g), h)), state_block],
        out_shape=[jax.ShapeDtypeStruct((N_TOK, A_W), F32),
                   jax.ShapeDtypeStruct((N_SEQ, H_A, HEAD_DIM, HEAD_DIM), F32)],
        scratch_shapes=[pltpu.VMEM((HG_HEADS, HEAD_DIM, HEAD_DIM), F32)],
        compiler_params=_params(("arbitrary", "arbitrary")),
        name="hgrn_scan_bwd" if reverse else "hgrn_scan_fwd",
    )(z, z, z, la, lc, om, sums, pairs, s0_all)


def _hgrn_finish_kernel(of_ref, ob_ref, og_ref, gn_ref, o_ref):
    o = of_ref[...] + ob_ref[...]
    og = og_ref[...]
    for h in range(H_A):
        cols = slice(h * HEAD_DIM, (h + 1) * HEAD_DIM)
        oh = o[:, cols]
        gh = og[:, cols]
        y = oh * lax.rsqrt(jnp.mean(oh * oh, axis=-1, keepdims=True) + EPS) * gn_ref[...]
        o_ref[:, cols] = (y * (gh * jax.nn.sigmoid(gh))).astype(o_ref.dtype)


def _hgrn_finish(o_f, o_b, z, gn_g):
    tm = 512
    blk = pl.BlockSpec((tm, A_W), lambda i: (i, 0))
    return pl.pallas_call(
        _hgrn_finish_kernel,
        grid=(N_TOK // tm,),
        in_specs=[blk, blk,
                  pl.BlockSpec((tm, A_W), lambda i: (i, 4)),
                  pl.BlockSpec((1, HEAD_DIM), lambda i: (0, 0))],
        out_specs=blk,
        out_shape=jax.ShapeDtypeStruct((N_TOK, A_W), BF16),
        compiler_params=_params(("arbitrary",)),
        name="hgrn_finish",
    )(o_f, o_b, z, gn_g.reshape(1, HEAD_DIM))


def _hgrn2(z, lb_f, lb_b, gn_g, state_l):
    outs, finals = [], []
    for reverse, lb in ((False, lb_f), (True, lb_b)):
        la = jnp.log(jnp.maximum(lb, LB_FLOOR)).reshape(1, A_W)
        lc = jnp.log1p(-lb).reshape(1, A_W)
        om = (1.0 - lb).reshape(1, A_W)
        s0_all = jnp.concatenate(
            [jnp.zeros((BATCH, H_A, HEAD_DIM, HEAD_DIM), F32), state_l[:, int(reverse)]], axis=0)
        o, sfin = _hgrn_scan(z, la, lc, om, s0_all, reverse)
        outs.append(o)
        finals.append(sfin[:BATCH])
    return _hgrn_finish(outs[0], outs[1], z, gn_g), jnp.stack(finals, axis=1)


CONV_HALO = 16


def _conv_kernel(prev_ref, cur_ref, next_ref, dw_ref, dwb_ref, lng_ref, lnb_ref, pw_ref, pwb_ref,
                 o_ref, ext_ref, pwb16_ref):
    i = pl.program_id(0)

    @pl.when(i == 0)
    def _():
        pwb16_ref[...] = pw_ref[...].astype(BF16)

    def glu(x):
        return x[:, :B_W] * jax.nn.sigmoid(x[:, B_W:])

    lat = i - CTX_SEGS
    pos = lax.rem(lat, LAT_SEGS_PER_SEQ)
    has_prev = jnp.logical_and(i >= CTX_SEGS, pos != 0)
    has_next = jnp.logical_and(i >= CTX_SEGS, pos != LAT_SEGS_PER_SEQ - 1)
    ext_ref[0:CONV_HALO, :] = jnp.where(has_prev, glu(prev_ref[HG_T - CONV_HALO:, :]), 0.0)
    ext_ref[CONV_HALO:CONV_HALO + HG_T, :] = glu(cur_ref[...])
    ext_ref[CONV_HALO + HG_T:, :] = jnp.where(has_next, glu(next_ref[:CONV_HALO, :]), 0.0)

    first = CONV_HALO - CONV_W // 2
    acc = ext_ref[first:first + HG_T, :] * dw_ref[0:1, :]
    for j in range(1, CONV_W):
        acc = acc + ext_ref[first + j:first + j + HG_T, :] * dw_ref[j:j + 1, :]
    u = acc + dwb_ref[...]
    mu = jnp.mean(u, axis=-1, keepdims=True)
    d = u - mu
    var = jnp.mean(d * d, axis=-1, keepdims=True)
    y = d * lax.rsqrt(var + EPS) * lng_ref[...] + lnb_ref[...]
    y = y * jax.nn.sigmoid(y)
    o = jnp.dot(y.astype(BF16), pwb16_ref[...], preferred_element_type=F32) + pwb_ref[...]
    o_ref[...] = o.astype(o_ref.dtype)


def _conv_module(glu_in, lw):
    row = lambda d: pl.BlockSpec((1, B_W), lambda i: (0, 0))
    seg = lambda off: pl.BlockSpec((HG_T, 2 * B_W), lambda i: (jnp.clip(i + off, 0, N_SEG - 1), 0))
    vec = lambda a: a.reshape(1, B_W)
    return pl.pallas_call(
        _conv_kernel,
        grid=(N_SEG,),
        in_specs=[seg(-1), seg(0), seg(1),
                  pl.BlockSpec((CONV_W, B_W), lambda i: (0, 0)), row(0), row(0), row(0),
                  pl.BlockSpec((B_W, B_W), lambda i: (0, 0)), row(0)],
        out_specs=pl.BlockSpec((HG_T, B_W), lambda i: (i, 0)),
        out_shape=jax.ShapeDtypeStruct((N_TOK, B_W), BF16),
        scratch_shapes=[pltpu.VMEM((HG_T + 2 * CONV_HALO, B_W), F32),
                        pltpu.VMEM((B_W, B_W), BF16)],
        compiler_params=_params(("arbitrary",)),
        name="conv_module",
    )(glu_in, glu_in, glu_in, lw['dw_w'], vec(lw['dw_b']), vec(lw['ln_g']), vec(lw['ln_b']),
      lw['pw_w'], vec(lw['pw_b']))


COL_GLU = 5 * H_A
COL_QC = COL_GLU + 2 * B_W // HEAD_DIM
COL_KC = COL_QC + H_C
COL_VC = COL_KC + H_C

NA_QROWS = 4
NA_QB = NA_QROWS * GRID_W
NA_KROWS = NA_QROWS + KH_MAX
NA_KEYS = NA_KROWS * GRID_W
NA_BLOCKS = DEC_SEQ // NA_QB
GRID_ROWS = DEC_SEQ // GRID_W


def _na_key_row0(first_query_row):
    clip = np.clip if isinstance(first_query_row, (int, np.integer)) else jnp.clip
    return clip(first_query_row - KH_MAX // 2, 0, GRID_ROWS - NA_KROWS)


def _na_bias_tables(rpb):
    kh = min(KH_MAX, GRID_ROWS)
    n_dr, n_dc = 2 * KH_MAX - 1, 2 * KW - 1
    c = np.arange(GRID_W)[:, None]
    kc = np.arange(GRID_W)[None, :]
    col_start = np.clip(c - KW // 2, 0, GRID_W - KW)
    col_valid = (kc >= col_start) & (kc < col_start + KW)
    dc = np.clip(kc - c, -(KW - 1), KW - 1) + KW - 1
    pick_dc = (dc.reshape(-1)[None, :] == np.arange(n_dc)[:, None]).astype(np.float32)
    pick_dr = np.zeros((4, NA_QROWS * NA_KROWS, n_dr), np.float32)
    row_valid = np.zeros((4, NA_QROWS, NA_KROWS), bool)
    for t, first_row in enumerate((0, NA_QROWS, 2 * NA_QROWS, GRID_ROWS - NA_QROWS)):
        r = first_row + np.arange(NA_QROWS)[:, None]
        kr = int(_na_key_row0(first_row)) + np.arange(NA_KROWS)[None, :]
        row_start = np.clip(r - kh // 2, 0, GRID_ROWS - kh)
        row_valid[t] = (kr >= row_start) & (kr < row_start + kh)
        dr = kr - r + KH_MAX - 1
        pick_dr[t] = (dr.reshape(-1)[:, None] == np.arange(n_dr)[None, :]) & row_valid[t].reshape(-1)[:, None]
    by_col = jnp.einsum('hrd,dx->hrx', rpb.astype(F32), pick_dc, precision=lax.Precision.HIGHEST)
    table = jnp.einsum('tpr,hrx->thpx', pick_dr, by_col, precision=lax.Precision.HIGHEST)
    table = table.reshape(4, H_C, NA_QROWS, NA_KROWS, GRID_W, GRID_W).transpose(0, 1, 2, 4, 3, 5)
    valid = row_valid[:, None, :, None, :, None] & col_valid[None, None, None, :, None, :]
    return jnp.where(valid, table, NEG_BIG).reshape(4, H_C, NA_QB, NA_KEYS)


def _softmax_pv(scores, values):
    m = functools.reduce(jnp.maximum, [jnp.max(s, axis=-1, keepdims=True) for s in scores])
    ps = [jnp.exp(s - m) for s in scores]
    denom = functools.reduce(jnp.add, [jnp.sum(p, axis=-1, keepdims=True) for p in ps])
    o = functools.reduce(jnp.add, [jnp.dot(p.astype(BF16), v, preferred_element_type=F32)
                                   for p, v in zip(ps, values)])
    return o / denom


def _na_kernel(q_ref, k_ref, v_ref, kc_ref, vc_ref, bias_ref, o_ref):
    key_row0 = _na_key_row0(pl.program_id(2) * NA_QROWS)
    keys = pl.ds(pl.multiple_of(key_row0 * GRID_W, NA_QB), NA_KEYS)
    scale = HEAD_DIM ** -0.5
    q = q_ref[...].astype(BF16)
    s_loc = _dot_nt(q, k_ref[keys, :].astype(BF16)) * scale + bias_ref[...]
    s_ctx = _dot_nt(q, kc_ref[...].astype(BF16)) * scale
    o = _softmax_pv([s_loc, s_ctx], [v_ref[keys, :].astype(BF16), vc_ref[...].astype(BF16)])
    o_ref[...] = o.astype(o_ref.dtype)


def _na_latent(z, cache_k, cache_v, rpb):
    bias = _na_bias_tables(rpb)
    q_blocks0 = N_CTX // NA_QB
    seq_blocks0 = N_CTX // DEC_SEQ

    def block_type(qb):
        return jnp.where(qb < 2, qb, jnp.where(qb == NA_BLOCKS - 1, 3, 2))

    seq_spec = lambda col0: pl.BlockSpec((DEC_SEQ, HEAD_DIM), lambda b, h, qb: (seq_blocks0 + b, col0 + h))
    ctx_spec = pl.BlockSpec((None, PAST_LEN, HEAD_DIM), lambda b, h, qb: (b, 0, h))
    return pl.pallas_call(
        _na_kernel,
        grid=(DEC_BATCH, H_C, NA_BLOCKS),
        in_specs=[pl.BlockSpec((NA_QB, HEAD_DIM), lambda b, h, qb: (q_blocks0 + b * NA_BLOCKS + qb, COL_QC + h)),
                  seq_spec(COL_KC), seq_spec(COL_VC), ctx_spec, ctx_spec,
                  pl.BlockSpec((None, None, NA_QB, NA_KEYS), lambda b, h, qb: (block_type(qb), h, 0, 0))],
        out_specs=pl.BlockSpec((NA_QB, HEAD_DIM), lambda b, h, qb: (b * NA_BLOCKS + qb, h)),
        out_shape=jax.ShapeDtypeStruct((N_LAT, C_W), BF16),
        compiler_params=_params(("arbitrary", "arbitrary", "arbitrary")),
        name="na_attention",
    )(z, z, z, cache_k.reshape(DEC_BATCH, PAST_LEN, C_W), cache_v.reshape(DEC_BATCH, PAST_LEN, C_W), bias)


def _ctx_attn_kernel(q_ref, k_ref, v_ref, k_all_ref, v_all_ref, o_ref, k_out_ref, v_out_ref):
    del k_all_ref, v_all_ref
    scale = HEAD_DIM ** -0.5
    k = k_ref[...]
    v = v_ref[...]
    k_out_ref[...] = k
    v_out_ref[...] = v
    s = _dot_nt(q_ref[...].astype(BF16), k.astype(BF16)) * scale
    o_ref[...] = _softmax_pv([s], [v.astype(BF16)]).astype(o_ref.dtype)


def _context_attention(z, k_all, v_all, layer):
    spec = lambda col0: pl.BlockSpec((SEQ, HEAD_DIM), lambda b, h: (b, col0 + h))
    per_head = pl.BlockSpec((None, None, None, SEQ, HEAD_DIM), lambda b, h: (b, layer, h, 0, 0))
    kv_shape = jax.ShapeDtypeStruct((BATCH, DEPTH, H_C, SEQ, HEAD_DIM), F32)
    whole = pl.BlockSpec(memory_space=pl.ANY)
    return pl.pallas_call(
        _ctx_attn_kernel,
        grid=(BATCH, H_C),
        in_specs=[spec(COL_QC), spec(COL_KC), spec(COL_VC), whole, whole],
        out_specs=[pl.BlockSpec((SEQ, HEAD_DIM), lambda b, h: (b, h)), per_head, per_head],
        out_shape=[jax.ShapeDtypeStruct((N_CTX, C_W), BF16), kv_shape, kv_shape],
        input_output_aliases={3: 1, 4: 2},
        compiler_params=_params(("arbitrary", "arbitrary")),
        name="context_attention",
    )(z, z, z, k_all, v_all)


def _mixers(z, lw, cache_k, cache_v, state_l, k_all, v_all, layer):
    o_a, state = _hgrn2(z, lw['lb_f'], lw['lb_b'], lw['gn_g'], state_l)
    o_b = _conv_module(z[:, 5 * A_W:5 * A_W + 2 * B_W], lw)
    o_ctx, k_all, v_all = _context_attention(z, k_all, v_all, layer)
    o_c = jnp.concatenate([o_ctx, _na_latent(z, cache_k, cache_v, lw['rpb'])], axis=0)
    return jnp.concatenate([o_a, o_b, o_c], axis=-1), k_all, v_all, state


def kernel(x_prompt, x_sample, cache_na_k, cache_na_v, state_hgrn, c, c_ctx, w_mod, b_mod,
           norm1_g, norm2_g, w_in, hgrn_lb, hgrn_gn_g, conv_dw_w, conv_dw_b, conv_ln_g,
           conv_ln_b, conv_pw_w, conv_pw_b, na_rpb, w_out, w_router, b_router, w_up, b_up,
           w_down, b_down, final_g):
    lb_soft = jax.nn.softmax(hgrn_lb.astype(F32), axis=1)
    lower_bounds = jnp.cumsum(lb_soft, axis=1) - lb_soft[:, :1]

    cond = jnp.zeros((8, D_MODEL), F32).at[0].set(c_ctx).at[1:1 + DEC_BATCH].set(c)
    mods = _modulation_all(cond, w_mod, b_mod)
    mods = mods[:, :N_SEQ_ROWS].reshape(DEPTH, N_SEQ_ROWS, 6, 1, D_MODEL).transpose(0, 2, 1, 3, 4)

    x = jnp.concatenate([x_prompt.reshape(N_CTX, D_MODEL), x_sample.reshape(N_LAT, D_MODEL)], axis=0)
    k_all = jnp.zeros((BATCH, DEPTH, H_C, SEQ, HEAD_DIM), F32)
    v_all = jnp.zeros((BATCH, DEPTH, H_C, SEQ, HEAD_DIM), F32)
    ss_new = []
    for l in range(DEPTH):
        sh1, sc1, g1, sh2, sc2, g2 = (mods[l, i] for i in range(6))
        lw = dict(lb_f=lower_bounds[0, l], lb_b=lower_bounds[1, l], gn_g=hgrn_gn_g[l],
                  dw_w=conv_dw_w[l], dw_b=conv_dw_b[l], ln_g=conv_ln_g[l], ln_b=conv_ln_b[l],
                  pw_w=conv_pw_w[l], pw_b=conv_pw_b[l], rpb=na_rpb[l])
        h = _norm_mod(x, norm1_g[l], sc1, sh1, BF16)
        z = _proj(h, w_in[l])
        mix, k_all, v_all, s_l = _mixers(z, lw, cache_na_k[:, l], cache_na_v[:, l], state_hgrn[:, l],
                                         k_all, v_all, l)
        ss_new.append(s_l)
        x = _proj_residual(mix, w_out[l], x, g1)

        h2, logits = _norm_mod_router(x, norm2_g[l], sc2, sh2, w_router[l], b_router[l])
        gates, dest, src_tok, blk_e, nsub, last_used = _route(logits[:, :N_EXPERTS])
        xs = _dispatch(h2, src_tok, nsub)
        y = _moe_experts(xs, blk_e, nsub, last_used, w_up[l], b_up[l], w_down[l], b_down[l])
        x = _combine(x, y, dest, gates, g2)

    y_all = _final_norm(x, final_g)
    y_prompt = y_all[:N_CTX].reshape(BATCH, SEQ, D_MODEL)
    y_sample = y_all[N_CTX:].reshape(DEC_BATCH, DEC_SEQ, D_MODEL)
    return (y_prompt, y_sample, k_all.transpose(0, 1, 3, 2, 4), v_all.transpose(0, 1, 3, 2, 4),
            jnp.stack(ss_new, axis=1))
```

```python
import functools

import jax
import jax.numpy as jnp
import numpy as np
from jax import lax
from jax.experimental import pallas as pl
from jax.experimental.pallas import tpu as pltpu

D_MODEL = 2048
BATCH = 32
SEQ = 256
DEPTH = 4
DEC_BATCH = 4
DEC_SEQ = 2048
PAST_LEN = 256
GRID_W = 64
HEAD_DIM = 128
A_W = 3 * D_MODEL // 8
H_A = A_W // HEAD_DIM
B_W = D_MODEL // 4
C_W = D_MODEL - A_W - B_W
H_C = C_W // HEAD_DIM
MIX_W = A_W + B_W + C_W
IN_COLS = 5 * A_W + 2 * B_W + 3 * C_W
CONV_W = 31
KH_MAX = 8
KW = 16
N_EXPERTS = 32
TOP_K = 4
D_FF = D_MODEL
SWIGLU_ALPHA = 1.702
SWIGLU_LIMIT = 7.0
EPS = 1e-6
NEG_BIG = -1e30
LB_FLOOR = 1e-30
F32 = jnp.float32
BF16 = jnp.bfloat16

N_CTX = BATCH * SEQ
N_LAT = DEC_BATCH * DEC_SEQ
N_TOK = N_CTX + N_LAT
N_SEQ_ROWS = 1 + DEC_BATCH

V7X_VMEM_LIMIT_BYTES = 56 * 1024 * 1024
V7X_VMEM_LIMIT_MOE_BYTES = 60 * 1024 * 1024

TM = 1024
TN = 1024
MOE_R = 1024
MOE_SUB = 256
MOE_TF = 512
MOE_L = N_TOK * TOP_K + N_EXPERTS * MOE_R
MOE_S = MOE_L // MOE_R
COMBINE_TOK = 256


def _seq_row(i, tm):
    n_ctx_blocks = N_CTX // tm
    per_seq = DEC_SEQ // tm
    return jnp.where(i < n_ctx_blocks, 0, 1 + (i - n_ctx_blocks) // per_seq)


def _params(sem, vmem_limit_bytes=V7X_VMEM_LIMIT_BYTES):
    return pltpu.CompilerParams(dimension_semantics=sem, vmem_limit_bytes=vmem_limit_bytes)


def _mod_kernel(c_ref, w_ref, b_ref, o_ref):
    s = c_ref[...]
    s = s * jax.nn.sigmoid(s)
    o_ref[...] = jnp.dot(s.astype(BF16), w_ref[...].astype(BF16),
                         preferred_element_type=F32) + b_ref[...]


def _modulation_all(cond, w_mod, b_mod):
    tn = 1024
    n = 6 * D_MODEL
    return pl.pallas_call(
        _mod_kernel,
        grid=(DEPTH, n // tn),
        in_specs=[
            pl.BlockSpec((8, D_MODEL), lambda l, j: (0, 0)),
            pl.BlockSpec((None, D_MODEL, tn), lambda l, j: (l, 0, j)),
            pl.BlockSpec((None, 1, tn), lambda l, j: (l, 0, j)),
        ],
        out_specs=pl.BlockSpec((None, 8, tn), lambda l, j: (l, 0, j)),
        out_shape=jax.ShapeDtypeStruct((DEPTH, 8, n), F32),
        compiler_params=_params(("arbitrary", "arbitrary")),
        name="modulation",
    )(cond, w_mod, b_mod.reshape(DEPTH, 1, n))


def _rms(x, g):
    return x * lax.rsqrt(jnp.mean(x * x, axis=-1, keepdims=True) + EPS) * g


def _norm_mod_kernel(x_ref, g_ref, sc_ref, sh_ref, o_ref):
    y = _rms(x_ref[...], g_ref[...])
    o_ref[...] = (y * (1.0 + sc_ref[...]) + sh_ref[...]).astype(o_ref.dtype)


def _norm_mod(x, g, sc, sh, dtype):
    tm = 512
    return pl.pallas_call(
        _norm_mod_kernel,
        grid=(N_TOK // tm,),
        in_specs=[
            pl.BlockSpec((tm, D_MODEL), lambda i: (i, 0)),
            pl.BlockSpec((1, D_MODEL), lambda i: (0, 0)),
            pl.BlockSpec((None, 1, D_MODEL), lambda i: (_seq_row(i, tm), 0, 0)),
            pl.BlockSpec((None, 1, D_MODEL), lambda i: (_seq_row(i, tm), 0, 0)),
        ],
        out_specs=pl.BlockSpec((tm, D_MODEL), lambda i: (i, 0)),
        out_shape=jax.ShapeDtypeStruct((N_TOK, D_MODEL), dtype),
        compiler_params=_params(("arbitrary",)),
        name="norm_mod",
    )(x, g.reshape(1, D_MODEL), sc, sh)


def _norm_mod_router_kernel(x_ref, g_ref, sc_ref, sh_ref, wr_ref, br_ref, o_ref, lg_ref):
    y = _rms(x_ref[...], g_ref[...])
    h = y * (1.0 + sc_ref[...]) + sh_ref[...]
    o_ref[...] = h.astype(o_ref.dtype)
    lg_ref[...] = jnp.dot(h, wr_ref[...], preferred_element_type=F32,
                          precision=lax.Precision.HIGHEST) + br_ref[...]


def _norm_mod_router(x, g, sc, sh, w_router, b_router):
    tm = 512
    wr = jnp.zeros((D_MODEL, 128), F32).at[:, :N_EXPERTS].set(w_router)
    br = jnp.zeros((1, 128), F32).at[0, :N_EXPERTS].set(b_router)
    return pl.pallas_call(
        _norm_mod_router_kernel,
        grid=(N_TOK // tm,),
        in_specs=[
            pl.BlockSpec((tm, D_MODEL), lambda i: (i, 0)),
            pl.BlockSpec((1, D_MODEL), lambda i: (0, 0)),
            pl.BlockSpec((None, 1, D_MODEL), lambda i: (_seq_row(i, tm), 0, 0)),
            pl.BlockSpec((None, 1, D_MODEL), lambda i: (_seq_row(i, tm), 0, 0)),
            pl.BlockSpec((D_MODEL, 128), lambda i: (0, 0)),
            pl.BlockSpec((1, 128), lambda i: (0, 0)),
        ],
        out_specs=[pl.BlockSpec((tm, D_MODEL), lambda i: (i, 0)),
                   pl.BlockSpec((tm, 128), lambda i: (i, 0))],
        out_shape=[jax.ShapeDtypeStruct((N_TOK, D_MODEL), F32),
                   jax.ShapeDtypeStruct((N_TOK, 128), F32)],
        compiler_params=_params(("arbitrary",)),
        name="norm_mod_router",
    )(x, g.reshape(1, D_MODEL), sc, sh, wr, br)


def _final_norm_kernel(x_ref, g_ref, o_ref):
    o_ref[...] = _rms(x_ref[...], g_ref[...])


def _final_norm(x, g):
    tm = 512
    return pl.pallas_call(
        _final_norm_kernel,
        grid=(N_TOK // tm,),
        in_specs=[pl.BlockSpec((tm, D_MODEL), lambda i: (i, 0)),
                  pl.BlockSpec((1, D_MODEL), lambda i: (0, 0))],
        out_specs=pl.BlockSpec((tm, D_MODEL), lambda i: (i, 0)),
        out_shape=jax.ShapeDtypeStruct((N_TOK, D_MODEL), F32),
        compiler_params=_params(("arbitrary",)),
        name="final_norm",
    )(x, g.reshape(1, D_MODEL))


def _proj_kernel(a_ref, w_ref, o_ref, wb_ref):
    @pl.when(pl.program_id(1) == 0)
    def _():
        wb_ref[...] = w_ref[...].astype(BF16)

    o_ref[...] = jnp.dot(a_ref[...], wb_ref[...], preferred_element_type=F32)


def _proj(a, w_all, layer):
    _, k, n = w_all.shape
    return pl.pallas_call(
        _proj_kernel,
        grid=(n // TN, N_TOK // TM),
        in_specs=[pl.BlockSpec((TM, k), lambda j, i: (i, 0)),
                  pl.BlockSpec((None, k, TN), lambda j, i: (layer, 0, j))],
        out_specs=pl.BlockSpec((TM, TN), lambda j, i: (i, j)),
        out_shape=jax.ShapeDtypeStruct((N_TOK, n), F32),
        scratch_shapes=[pltpu.VMEM((k, TN), BF16)],
        compiler_params=_params(("arbitrary", "arbitrary")),
        name="proj",
    )(a, w_all)


def _proj_residual_kernel(a_ref, w_ref, x_ref, gate_ref, o_ref, wb_ref):
    @pl.when(pl.program_id(1) == 0)
    def _():
        wb_ref[...] = w_ref[...].astype(BF16)

    y = jnp.dot(a_ref[...], wb_ref[...], preferred_element_type=F32)
    o_ref[...] = x_ref[...] + gate_ref[...] * y


def _proj_residual(a, w_all, layer, x, gate):
    _, k, n = w_all.shape
    return pl.pallas_call(
        _proj_residual_kernel,
        grid=(n // TN, N_TOK // TM),
        in_specs=[pl.BlockSpec((TM, k), lambda j, i: (i, 0)),
                  pl.BlockSpec((None, k, TN), lambda j, i: (layer, 0, j)),
                  pl.BlockSpec((TM, TN), lambda j, i: (i, j)),
                  pl.BlockSpec((None, 1, TN), lambda j, i: (_seq_row(i, TM), 0, j))],
        out_specs=pl.BlockSpec((TM, TN), lambda j, i: (i, j)),
        out_shape=jax.ShapeDtypeStruct((N_TOK, n), F32),
        scratch_shapes=[pltpu.VMEM((k, TN), BF16)],
        compiler_params=_params(("arbitrary", "arbitrary")),
        name="proj_residual",
    )(a, w_all, x, gate)


def _swiglu(g, u):
    g = jnp.minimum(g, SWIGLU_LIMIT)
    u = jnp.clip(u, -SWIGLU_LIMIT, SWIGLU_LIMIT)
    return g * jax.nn.sigmoid(SWIGLU_ALPHA * g) * (u + 1.0)


def _moe_kernel(blk_e_ref, nsub_ref, last_ref, xs_ref, wg_ref, wu_ref, wd_ref, bg_ref, bu_ref, bd_ref,
                o_ref, wgb_ref, wub_ref, wdb_ref):
    s = pl.program_id(0)
    f = pl.program_id(1)
    n = nsub_ref[s]

    def contribution(group, n_groups):
        rows = pl.ds(pl.multiple_of(group * MOE_SUB, MOE_SUB), n_groups * MOE_SUB)
        x = xs_ref[rows, :]
        g = jnp.dot(x, wgb_ref[...], preferred_element_type=F32) + bg_ref[...]
        u = jnp.dot(x, wub_ref[...], preferred_element_type=F32) + bu_ref[...]
        a = _swiglu(g, u).astype(BF16)
        return rows, jnp.dot(a, wdb_ref[...], preferred_element_type=F32)

    def first_chunk(group, n_groups):
        rows, y = contribution(group, n_groups)
        o_ref[rows, :] = y + bd_ref[...]

    def later_chunk(group, n_groups):
        rows, y = contribution(group, n_groups)
        o_ref[rows, :] += y

    def sweep(update):
        def pair(m, carry):
            update(2 * m, 2)
            return carry
        lax.fori_loop(0, n // 2, pair, 0)

        @pl.when(n % 2 == 1)
        def _():
            update(n - 1, 1)

    @pl.when(n > 0)
    def _():
        wgb_ref[...] = wg_ref[...].astype(BF16)
        wub_ref[...] = wu_ref[...].astype(BF16)
        wdb_ref[...] = wd_ref[...].astype(BF16)

        @pl.when(f == 0)
        def _():
            sweep(first_chunk)

        @pl.when(f > 0)
        def _():
            sweep(later_chunk)


def _moe_experts(xs, blk_e, nsub, last_used, layer, w_up, b_up, w_down, b_down):
    nf = D_FF // MOE_TF

    def blk(s, last):
        return jnp.minimum(s, last[0])

    grid_spec = pltpu.PrefetchScalarGridSpec(
        num_scalar_prefetch=3,
        grid=(MOE_S, nf),
        in_specs=[
            pl.BlockSpec((MOE_R, D_MODEL), lambda s, f, e, n, last: (blk(s, last), 0)),
            pl.BlockSpec((None, None, D_MODEL, MOE_TF),
                         lambda s, f, e, n, last: (layer, e[s], 0, jnp.where(n[s] > 0, f, nf - 1))),
            pl.BlockSpec((None, None, D_MODEL, MOE_TF),
                         lambda s, f, e, n, last: (layer, e[s], 0, nf + jnp.where(n[s] > 0, f, nf - 1))),
            pl.BlockSpec((None, None, MOE_TF, D_MODEL),
                         lambda s, f, e, n, last: (layer, e[s], jnp.where(n[s] > 0, f, nf - 1), 0)),
            pl.BlockSpec((None, None, 1, MOE_TF),
                         lambda s, f, e, n, last: (layer, e[s], 0, jnp.where(n[s] > 0, f, nf - 1))),
            pl.BlockSpec((None, None, 1, MOE_TF),
                         lambda s, f, e, n, last: (layer, e[s], 0, nf + jnp.where(n[s] > 0, f, nf - 1))),
            pl.BlockSpec((None, None, 1, D_MODEL), lambda s, f, e, n, last: (layer, e[s], 0, 0)),
        ],
        out_specs=pl.BlockSpec((MOE_R, D_MODEL), lambda s, f, e, n, last: (blk(s, last), 0)),
        scratch_shapes=[pltpu.VMEM((D_MODEL, MOE_TF), BF16),
                        pltpu.VMEM((D_MODEL, MOE_TF), BF16),
                        pltpu.VMEM((MOE_TF, D_MODEL), BF16)],
    )
    return pl.pallas_call(
        _moe_kernel,
        grid_spec=grid_spec,
        out_shape=jax.ShapeDtypeStruct((MOE_L, D_MODEL), F32),
        compiler_params=_params(("arbitrary", "arbitrary"), V7X_VMEM_LIMIT_MOE_BYTES),
        name="moe_experts",
    )(blk_e, nsub, last_used, xs, w_up, w_up, w_down,
      b_up.reshape(DEPTH, N_EXPERTS, 1, 2 * D_FF), b_up.reshape(DEPTH, N_EXPERTS, 1, 2 * D_FF),
      b_down.reshape(DEPTH, N_EXPERTS, 1, D_MODEL))


def _route(logits):
    top_val, top_idx = lax.top_k(logits, TOP_K)
    gates = jax.nn.softmax(top_val, axis=-1)
    flat_e = top_idx.reshape(-1)
    onehot = (flat_e[:, None] == jnp.arange(N_EXPERTS)[None, :]).astype(jnp.int32)
    rank = jnp.sum((jnp.cumsum(onehot, axis=0) - onehot) * onehot, axis=1)
    counts = jnp.sum(onehot, axis=0)
    padded = (counts + MOE_R - 1) // MOE_R * MOE_R
    pad_ends = jnp.cumsum(padded)
    pad_starts = pad_ends - padded
    dest = (pad_starts[flat_e] + rank).astype(jnp.int32)
    blk_start = jnp.arange(MOE_S, dtype=jnp.int32) * MOE_R
    blk_e = jnp.minimum(jnp.searchsorted(pad_ends, blk_start, side='right'),
                        N_EXPERTS - 1).astype(jnp.int32)
    used = blk_start < pad_ends[-1]
    in_blk = jnp.clip(pad_starts[blk_e] + counts[blk_e] - blk_start, 0, MOE_R)
    nsub = jnp.where(used, (in_blk + MOE_SUB - 1) // MOE_SUB, 0).astype(jnp.int32)
    last_used = (pad_ends[-1] // MOE_R - 1).astype(jnp.int32).reshape(1)
    blk_e = jnp.where(used, blk_e, blk_e[last_used[0]])
    tok = jnp.arange(N_TOK * TOP_K, dtype=jnp.int32) // TOP_K
    src_tok = jnp.zeros((MOE_L,), jnp.int32).at[dest].set(tok)
    return gates, dest, src_tok, blk_e, nsub, last_used


def _row_copy(src_hbm, src_row, dst, dst_row, sem):
    return pltpu.make_async_copy(src_hbm.at[pl.ds(src_row, 1)], dst.at[pl.ds(dst_row, 1)], sem)


ROW_ISSUE_UNROLL = 8


def _dispatch_kernel(nsub_ref, src_ref, src_next_ref, h_hbm, o_ref, buf_ref, sems):
    i = pl.program_id(0)
    n_steps = pl.num_programs(0)
    slot = i % 2

    def issue(step, s_ref, to_slot):
        def body(c, carry):
            for u in range(ROW_ISSUE_UNROLL):
                r = c * ROW_ISSUE_UNROLL + u
                _row_copy(h_hbm, s_ref[0, r], buf_ref.at[to_slot], r, sems.at[to_slot]).start(priority=u % 2)
            return carry
        lax.fori_loop(0, nsub_ref[step] * (MOE_SUB // ROW_ISSUE_UNROLL), body, 0)

    @pl.when(i == 0)
    def _():
        issue(0, src_ref, 0)

    @pl.when(i + 1 < n_steps)
    def _():
        issue(i + 1, src_next_ref, 1 - slot)

    for g in range(MOE_R // MOE_SUB):
        @pl.when(g < nsub_ref[i])
        def _():
            pltpu.make_async_copy(h_hbm.at[pl.ds(0, MOE_SUB)], buf_ref.at[slot, pl.ds(0, MOE_SUB)],
                                  sems.at[slot]).wait()

    for g in range(MOE_R // MOE_SUB):
        @pl.when(g < nsub_ref[i])
        def _():
            rows = pl.ds(g * MOE_SUB, MOE_SUB)
            o_ref[rows, :] = buf_ref[slot, rows, :].astype(o_ref.dtype)


def _dispatch(h, src_tok, nsub):
    grid_spec = pltpu.PrefetchScalarGridSpec(
        num_scalar_prefetch=1,
        grid=(MOE_S,),
        in_specs=[
            pl.BlockSpec((None, 1, MOE_R), lambda i, n: (i, 0, 0), memory_space=pltpu.SMEM),
            pl.BlockSpec((None, 1, MOE_R), lambda i, n: (jnp.minimum(i + 1, MOE_S - 1), 0, 0),
                         memory_space=pltpu.SMEM),
            pl.BlockSpec(memory_space=pl.ANY),
        ],
        out_specs=pl.BlockSpec((MOE_R, D_MODEL), lambda i, n: (i, 0)),
        scratch_shapes=[pltpu.VMEM((2, MOE_R, D_MODEL), F32),
                        pltpu.SemaphoreType.DMA((2,))],
    )
    src = src_tok.reshape(MOE_S, 1, MOE_R)
    return pl.pallas_call(
        _dispatch_kernel,
        grid_spec=grid_spec,
        out_shape=jax.ShapeDtypeStruct((MOE_L, D_MODEL), BF16),
        compiler_params=_params(("arbitrary",)),
        name="moe_dispatch",
    )(nsub, src, src, h)


def _combine_kernel(dest_ref, dest_next_ref, x_ref, y_hbm, w_ref, gate_ref, o_ref, buf_ref, sems):
    i = pl.program_id(0)
    n_steps = pl.num_programs(0)
    slot = i % 2

    def issue(d_ref, to_slot):
        for k in range(TOP_K):
            def body(c, carry):
                for u in range(ROW_ISSUE_UNROLL):
                    t = c * ROW_ISSUE_UNROLL + u
                    _row_copy(y_hbm, d_ref[0, k * COMBINE_TOK + t], buf_ref.at[to_slot, k], t,
                              sems.at[to_slot]).start(priority=u % 2)
                return carry
            lax.fori_loop(0, COMBINE_TOK // ROW_ISSUE_UNROLL, body, 0)

    @pl.when(i == 0)
    def _():
        issue(dest_ref, 0)

    @pl.when(i + 1 < n_steps)
    def _():
        issue(dest_next_ref, 1 - slot)

    for k in range(TOP_K):
        pltpu.make_async_copy(y_hbm.at[pl.ds(0, COMBINE_TOK)], buf_ref.at[slot, k], sems.at[slot]).wait()

    w = w_ref[...]
    acc = w[:, 0:1] * buf_ref[slot, 0]
    for k in range(1, TOP_K):
        acc = acc + w[:, k:k + 1] * buf_ref[slot, k]
    o_ref[...] = x_ref[...] + gate_ref[...] * acc


def _combine(x, y, dest, gates, gate2):
    tm = COMBINE_TOK
    n_steps = N_TOK // tm
    gp = jnp.zeros((N_TOK, 128), F32).at[:, :TOP_K].set(gates)
    dest_km = dest.reshape(n_steps, tm, TOP_K).transpose(0, 2, 1).reshape(n_steps, 1, TOP_K * tm)
    return pl.pallas_call(
        _combine_kernel,
        grid=(n_steps,),
        in_specs=[pl.BlockSpec((None, 1, TOP_K * tm), lambda i: (i, 0, 0), memory_space=pltpu.SMEM),
                  pl.BlockSpec((None, 1, TOP_K * tm), lambda i: (jnp.minimum(i + 1, n_steps - 1), 0, 0),
                               memory_space=pltpu.SMEM),
                  pl.BlockSpec((tm, D_MODEL), lambda i: (i, 0)),
                  pl.BlockSpec(memory_space=pl.ANY),
                  pl.BlockSpec((tm, 128), lambda i: (i, 0)),
                  pl.BlockSpec((None, 1, D_MODEL), lambda i: (_seq_row(i, tm), 0, 0))],
        out_specs=pl.BlockSpec((tm, D_MODEL), lambda i: (i, 0)),
        out_shape=jax.ShapeDtypeStruct((N_TOK, D_MODEL), F32),
        scratch_shapes=[pltpu.VMEM((2, TOP_K, tm, D_MODEL), F32),
                        pltpu.SemaphoreType.DMA((2,))],
        compiler_params=_params(("arbitrary",)),
        name="moe_combine",
    )(dest_km, dest_km, x, y, gp, gate2)


HG_T = 256
HG_LEVELS = 8
N_SEG = N_TOK // HG_T
CTX_SEGS = N_CTX // HG_T
LAT_SEGS_PER_SEQ = DEC_SEQ // HG_T
N_SEQ = BATCH + DEC_BATCH


def _hgrn_tables(reverse):
    t = np.arange(HG_T)
    r, j = t[:, None], t[None, :]
    sums = np.zeros((HG_LEVELS + 2, HG_T, HG_T), np.float32)
    pairs = np.zeros((HG_LEVELS + 1, HG_T, HG_T), np.float32)
    for l in range(HG_LEVELS):
        same_block = (r >> l) == (j >> l)
        later_half = ((r >> l) & 1) == 1
        sums[l] = same_block & np.where(later_half, j <= r, j > r)
        pairs[l] = ((r >> (l + 1)) == (j >> (l + 1))) & later_half & (((j >> l) & 1) == 0)
    sums[HG_LEVELS] = j <= r
    sums[HG_LEVELS + 1] = j > r
    pairs[HG_LEVELS] = r == j
    if reverse:
        sums = sums[:, ::-1, ::-1]
        pairs = pairs[:, ::-1, ::-1]
    return jnp.asarray(sums, BF16), jnp.asarray(pairs, F32)


def _dot_nt(x, y):
    return lax.dot_general(x, y, (((1,), (1,)), ((), ())), preferred_element_type=F32)


def _dot_tn(x, y):
    return lax.dot_general(x, y, (((0,), (0,)), ((), ())), preferred_element_type=F32)


def _hgrn_segment(g, reverse):
    seg = (N_SEG - 1 - g) if reverse else g
    lat = seg - CTX_SEGS
    first = (LAT_SEGS_PER_SEQ - 1) if reverse else 0
    starts = jnp.logical_or(seg < CTX_SEGS, lax.rem(lat, LAT_SEGS_PER_SEQ) == first)
    seq = jnp.where(seg < CTX_SEGS, seg, CTX_SEGS + lat // LAT_SEGS_PER_SEQ)
    return seg, seq, starts


HG_HEADS = 3


def _hgrn_head(qr, vr, z, la, lc, om, sums_ref, pairs_ref, state):
    q = qr * jax.nn.sigmoid(qr) * HEAD_DIM ** -0.5
    log_sig = jnp.minimum(z, 0.0) - jnp.log1p(jnp.exp(-jnp.abs(z)))
    b = lc + log_sig
    log_f = jnp.maximum(la, b) + jnp.log1p(jnp.exp(-jnp.abs(la - b)))
    k = om * jax.nn.sigmoid(-z)
    v = vr.astype(BF16)
    hi = log_f.astype(BF16)
    lo = (log_f - hi.astype(F32)).astype(BF16)
    parts = jnp.concatenate([hi, lo], axis=1)

    def block_sum(i):
        s2 = jnp.dot(sums_ref[i], parts, preferred_element_type=F32)
        return s2[:, :HEAD_DIM] + s2[:, HEAD_DIM:]

    scores = pairs_ref[HG_LEVELS] * _dot_nt(q.astype(BF16), k.astype(BF16))
    for l in range(HG_LEVELS):
        e = jnp.exp(block_sum(l))
        scores = scores + pairs_ref[l] * _dot_nt((q * e).astype(BF16), (k * e).astype(BF16))
    o = jnp.dot(scores.astype(BF16), v, preferred_element_type=F32)

    q_dec = (q * jnp.exp(block_sum(HG_LEVELS))).astype(BF16)
    o = o + jnp.dot(q_dec, state.astype(BF16), preferred_element_type=F32)

    k_dec = (k * jnp.exp(block_sum(HG_LEVELS + 1))).astype(BF16)
    total = _dot_tn(parts, jnp.ones((HG_T, HEAD_DIM), BF16))
    decay = jnp.exp(total[:HEAD_DIM] + total[HEAD_DIM:])
    return o, state * decay + _dot_tn(k_dec, v)


def _hgrn_kernel(q_ref, v_ref, z_ref, la_ref, lc_ref, om_ref, sums_ref, pairs_ref, s0_ref,
                 o_ref, sfin_ref, st_ref, *, reverse):
    _, _, starts = _hgrn_segment(pl.program_id(1), reverse)

    @pl.when(starts)
    def _():
        st_ref[...] = s0_ref[...]

    for h in range(HG_HEADS):
        cols = slice(h * HEAD_DIM, (h + 1) * HEAD_DIM)
        o, new_state = _hgrn_head(q_ref[:, cols], v_ref[:, cols], z_ref[:, cols], la_ref[:, cols],
                                  lc_ref[:, cols], om_ref[:, cols], sums_ref, pairs_ref, st_ref[h])
        o_ref[:, cols] = o
        st_ref[h] = new_state
        sfin_ref[h] = new_state


def _hgrn_scan(z, la, lc, om, s0_all, reverse):
    sums, pairs = _hgrn_tables(reverse)
    width = HG_HEADS * HEAD_DIM
    groups = H_A // HG_HEADS
    gate_col = (3 if reverse else 2) * groups

    def seg_of(g):
        return _hgrn_segment(g, reverse)[0]

    def seq_of(g):
        return _hgrn_segment(g, reverse)[1]

    row_block = lambda col0: pl.BlockSpec((HG_T, width), lambda h, g: (seg_of(g), col0 + h))
    chan = pl.BlockSpec((1, width), lambda h, g: (0, h))
    state_block = pl.BlockSpec((None, HG_HEADS, HEAD_DIM, HEAD_DIM), lambda h, g: (seq_of(g), h, 0, 0))
    return pl.pallas_call(
        functools.partial(_hgrn_kernel, reverse=reverse),
        grid=(groups, N_SEG),
        in_specs=[row_block(0), row_block(groups), row_block(gate_col), chan, chan, chan,
                  pl.BlockSpec(sums.shape, lambda h, g: (0, 0, 0)),
                  pl.BlockSpec(pairs.shape, lambda h, g: (0, 0, 0)),
                  state_block],
        out_specs=[pl.BlockSpec((HG_T, width), lambda h, g: (seg_of(g), h)), state_block],
        out_shape=[jax.ShapeDtypeStruct((N_TOK, A_W), F32),
                   jax.ShapeDtypeStruct((N_SEQ, H_A, HEAD_DIM, HEAD_DIM), F32)],
        scratch_shapes=[pltpu.VMEM((HG_HEADS, HEAD_DIM, HEAD_DIM), F32)],
        compiler_params=_params(("arbitrary", "arbitrary")),
        name="hgrn_scan_bwd" if reverse else "hgrn_scan_fwd",
    )(z, z, z, la, lc, om, sums, pairs, s0_all)


def _hgrn_finish_kernel(of_ref, ob_ref, og_ref, gn_ref, o_ref):
    o = of_ref[...] + ob_ref[...]
    og = og_ref[...]
    for h in range(H_A):
        cols = slice(h * HEAD_DIM, (h + 1) * HEAD_DIM)
        oh = o[:, cols]
        gh = og[:, cols]
        y = oh * lax.rsqrt(jnp.mean(oh * oh, axis=-1, keepdims=True) + EPS) * gn_ref[...]
        o_ref[:, cols] = (y * (gh * jax.nn.sigmoid(gh))).astype(o_ref.dtype)


def _hgrn_finish(o_f, o_b, z, gn_g):
    tm = 512
    blk = pl.BlockSpec((tm, A_W), lambda i: (i, 0))
    return pl.pallas_call(
        _hgrn_finish_kernel,
        grid=(N_TOK // tm,),
        in_specs=[blk, blk,
                  pl.BlockSpec((tm, A_W), lambda i: (i, 4)),
                  pl.BlockSpec((1, HEAD_DIM), lambda i: (0, 0))],
        out_specs=blk,
        out_shape=jax.ShapeDtypeStruct((N_TOK, A_W), BF16),
        compiler_params=_params(("arbitrary",)),
        name="hgrn_finish",
    )(o_f, o_b, z, gn_g.reshape(1, HEAD_DIM))


def _hgrn2(z, lb_f, lb_b, gn_g, state_l):
    outs, finals = [], []
    for reverse, lb in ((False, lb_f), (True, lb_b)):
        la = jnp.log(jnp.maximum(lb, LB_FLOOR)).reshape(1, A_W)
        lc = jnp.log1p(-lb).reshape(1, A_W)
        om = (1.0 - lb).reshape(1, A_W)
        s0_all = jnp.concatenate(
            [jnp.zeros((BATCH, H_A, HEAD_DIM, HEAD_DIM), F32), state_l[:, int(reverse)]], axis=0)
        o, sfin = _hgrn_scan(z, la, lc, om, s0_all, reverse)
        outs.append(o)
        finals.append(sfin[:BATCH])
    return _hgrn_finish(outs[0], outs[1], z, gn_g), jnp.stack(finals, axis=1)


CONV_HALO = 16


def _conv_kernel(prev_ref, cur_ref, next_ref, dw_ref, dwb_ref, lng_ref, lnb_ref, pw_ref, pwb_ref,
                 o_ref, ext_ref, pwb16_ref):
    i = pl.program_id(0)

    @pl.when(i == 0)
    def _():
        pwb16_ref[...] = pw_ref[...].astype(BF16)

    def glu(x):
        return x[:, :B_W] * jax.nn.sigmoid(x[:, B_W:])

    lat = i - CTX_SEGS
    pos = lax.rem(lat, LAT_SEGS_PER_SEQ)
    has_prev = jnp.logical_and(i >= CTX_SEGS, pos != 0)
    has_next = jnp.logical_and(i >= CTX_SEGS, pos != LAT_SEGS_PER_SEQ - 1)
    ext_ref[0:CONV_HALO, :] = jnp.where(has_prev, glu(prev_ref[HG_T - CONV_HALO:, :]), 0.0)
    ext_ref[CONV_HALO:CONV_HALO + HG_T, :] = glu(cur_ref[...])
    ext_ref[CONV_HALO + HG_T:, :] = jnp.where(has_next, glu(next_ref[:CONV_HALO, :]), 0.0)

    first = CONV_HALO - CONV_W // 2
    acc = ext_ref[first:first + HG_T, :] * dw_ref[0:1, :]
    for j in range(1, CONV_W):
        acc = acc + ext_ref[first + j:first + j + HG_T, :] * dw_ref[j:j + 1, :]
    u = acc + dwb_ref[...]
    mu = jnp.mean(u, axis=-1, keepdims=True)
    d = u - mu
    var = jnp.mean(d * d, axis=-1, keepdims=True)
    y = d * lax.rsqrt(var + EPS) * lng_ref[...] + lnb_ref[...]
    y = y * jax.nn.sigmoid(y)
    o = jnp.dot(y.astype(BF16), pwb16_ref[...], preferred_element_type=F32) + pwb_ref[...]
    o_ref[...] = o.astype(o_ref.dtype)


def _conv_module(glu_in, lw):
    row = lambda d: pl.BlockSpec((1, B_W), lambda i: (0, 0))
    seg = lambda off: pl.BlockSpec((HG_T, 2 * B_W), lambda i: (jnp.clip(i + off, 0, N_SEG - 1), 0))
    vec = lambda a: a.reshape(1, B_W)
    return pl.pallas_call(
        _conv_kernel,
        grid=(N_SEG,),
        in_specs=[seg(-1), seg(0), seg(1),
                  pl.BlockSpec((CONV_W, B_W), lambda i: (0, 0)), row(0), row(0), row(0),
                  pl.BlockSpec((B_W, B_W), lambda i: (0, 0)), row(0)],
        out_specs=pl.BlockSpec((HG_T, B_W), lambda i: (i, 0)),
        out_shape=jax.ShapeDtypeStruct((N_TOK, B_W), BF16),
        scratch_shapes=[pltpu.VMEM((HG_T + 2 * CONV_HALO, B_W), F32),
                        pltpu.VMEM((B_W, B_W), BF16)],
        compiler_params=_params(("arbitrary",)),
        name="conv_module",
    )(glu_in, glu_in, glu_in, lw['dw_w'], vec(lw['dw_b']), vec(lw['ln_g']), vec(lw['ln_b']),
      lw['pw_w'], vec(lw['pw_b']))


COL_GLU = 5 * H_A
COL_QC = COL_GLU + 2 * B_W // HEAD_DIM
COL_KC = COL_QC + H_C
COL_VC = COL_KC + H_C

NA_QROWS = 4
NA_QB = NA_QROWS * GRID_W
NA_KROWS = NA_QROWS + KH_MAX
NA_KEYS = NA_KROWS * GRID_W
NA_BLOCKS = DEC_SEQ // NA_QB
GRID_ROWS = DEC_SEQ // GRID_W


def _na_key_row0(first_query_row):
    clip = np.clip if isinstance(first_query_row, (int, np.integer)) else jnp.clip
    return clip(first_query_row - KH_MAX // 2, 0, GRID_ROWS - NA_KROWS)


def _na_bias_tables(rpb):
    kh = min(KH_MAX, GRID_ROWS)
    n_dr, n_dc = 2 * KH_MAX - 1, 2 * KW - 1
    c = np.arange(GRID_W)[:, None]
    kc = np.arange(GRID_W)[None, :]
    col_start = np.clip(c - KW // 2, 0, GRID_W - KW)
    col_valid = (kc >= col_start) & (kc < col_start + KW)
    dc = np.clip(kc - c, -(KW - 1), KW - 1) + KW - 1
    pick_dc = (dc.reshape(-1)[None, :] == np.arange(n_dc)[:, None]).astype(np.float32)
    pick_dr = np.zeros((4, NA_QROWS * NA_KROWS, n_dr), np.float32)
    row_valid = np.zeros((4, NA_QROWS, NA_KROWS), bool)
    for t, first_row in enumerate((0, NA_QROWS, 2 * NA_QROWS, GRID_ROWS - NA_QROWS)):
        r = first_row + np.arange(NA_QROWS)[:, None]
        kr = int(_na_key_row0(first_row)) + np.arange(NA_KROWS)[None, :]
        row_start = np.clip(r - kh // 2, 0, GRID_ROWS - kh)
        row_valid[t] = (kr >= row_start) & (kr < row_start + kh)
        dr = kr - r + KH_MAX - 1
        pick_dr[t] = (dr.reshape(-1)[:, None] == np.arange(n_dr)[None, :]) & row_valid[t].reshape(-1)[:, None]
    by_col = jnp.einsum('hrd,dx->hrx', rpb.astype(F32), pick_dc, precision=lax.Precision.HIGHEST)
    table = jnp.einsum('tpr,hrx->thpx', pick_dr, by_col, precision=lax.Precision.HIGHEST)
    table = table.reshape(4, H_C, NA_QROWS, NA_KROWS, GRID_W, GRID_W).transpose(0, 1, 2, 4, 3, 5)
    valid = row_valid[:, None, :, None, :, None] & col_valid[None, None, None, :, None, :]
    return jnp.where(valid, table, NEG_BIG).reshape(4, H_C, NA_QB, NA_KEYS)


def _softmax_pv(scores, values):
    m = functools.reduce(jnp.maximum, [jnp.max(s, axis=-1, keepdims=True) for s in scores])
    ps = [jnp.exp(s - m) for s in scores]
    denom = functools.reduce(jnp.add, [jnp.sum(p, axis=-1, keepdims=True) for p in ps])
    o = functools.reduce(jnp.add, [jnp.dot(p.astype(BF16), v, preferred_element_type=F32)
                                   for p, v in zip(ps, values)])
    return o / denom


def _na_kernel(q_ref, k_ref, v_ref, kc_ref, vc_ref, bias_ref, o_ref):
    key_row0 = _na_key_row0(pl.program_id(2) * NA_QROWS)
    keys = pl.ds(pl.multiple_of(key_row0 * GRID_W, NA_QB), NA_KEYS)
    scale = HEAD_DIM ** -0.5
    q = q_ref[...].astype(BF16)
    s_loc = _dot_nt(q, k_ref[keys, :].astype(BF16)) * scale + bias_ref[...]
    s_ctx = _dot_nt(q, kc_ref[...].astype(BF16)) * scale
    o = _softmax_pv([s_loc, s_ctx], [v_ref[keys, :].astype(BF16), vc_ref[...].astype(BF16)])
    o_ref[...] = o.astype(o_ref.dtype)


def _na_latent(z, cache_k, cache_v, rpb):
    bias = _na_bias_tables(rpb)
    q_blocks0 = N_CTX // NA_QB
    seq_blocks0 = N_CTX // DEC_SEQ

    def block_type(qb):
        return jnp.where(qb < 2, qb, jnp.where(qb == NA_BLOCKS - 1, 3, 2))

    seq_spec = lambda col0: pl.BlockSpec((DEC_SEQ, HEAD_DIM), lambda b, h, qb: (seq_blocks0 + b, col0 + h))
    ctx_spec = pl.BlockSpec((None, PAST_LEN, HEAD_DIM), lambda b, h, qb: (b, 0, h))
    return pl.pallas_call(
        _na_kernel,
        grid=(DEC_BATCH, H_C, NA_BLOCKS),
        in_specs=[pl.BlockSpec((NA_QB, HEAD_DIM), lambda b, h, qb: (q_blocks0 + b * NA_BLOCKS + qb, COL_QC + h)),
                  seq_spec(COL_KC), seq_spec(COL_VC), ctx_spec, ctx_spec,
                  pl.BlockSpec((None, None, NA_QB, NA_KEYS), lambda b, h, qb: (block_type(qb), h, 0, 0))],
        out_specs=pl.BlockSpec((NA_QB, HEAD_DIM), lambda b, h, qb: (b * NA_BLOCKS + qb, h)),
        out_shape=jax.ShapeDtypeStruct((N_LAT, C_W), BF16),
        compiler_params=_params(("arbitrary", "arbitrary", "arbitrary")),
        name="na_attention",
    )(z, z, z, cache_k.reshape(DEC_BATCH, PAST_LEN, C_W), cache_v.reshape(DEC_BATCH, PAST_LEN, C_W), bias)


def _ctx_attn_kernel(q_ref, k_ref, v_ref, k_all_ref, v_all_ref, o_ref, k_out_ref, v_out_ref):
    del k_all_ref, v_all_ref
    scale = HEAD_DIM ** -0.5
    k = k_ref[...]
    v = v_ref[...]
    k_out_ref[...] = k
    v_out_ref[...] = v
    s = _dot_nt(q_ref[...].astype(BF16), k.astype(BF16)) * scale
    o_ref[...] = _softmax_pv([s], [v.astype(BF16)]).astype(o_ref.dtype)


def _context_attention(z, k_all, v_all, layer):
    spec = lambda col0: pl.BlockSpec((SEQ, HEAD_DIM), lambda b, h: (b, col0 + h))
    per_head = pl.BlockSpec((None, None, None, SEQ, HEAD_DIM), lambda b, h: (b, layer, h, 0, 0))
    kv_shape = jax.ShapeDtypeStruct((BATCH, DEPTH, H_C, SEQ, HEAD_DIM), F32)
    whole = pl.BlockSpec(memory_space=pl.ANY)
    return pl.pallas_call(
        _ctx_attn_kernel,
        grid=(BATCH, H_C),
        in_specs=[spec(COL_QC), spec(COL_KC), spec(COL_VC), whole, whole],
        out_specs=[pl.BlockSpec((SEQ, HEAD_DIM), lambda b, h: (b, h)), per_head, per_head],
        out_shape=[jax.ShapeDtypeStruct((N_CTX, C_W), BF16), kv_shape, kv_shape],
        input_output_aliases={3: 1, 4: 2},
        compiler_params=_params(("arbitrary", "arbitrary")),
        name="context_attention",
    )(z, z, z, k_all, v_all)


def _mixers(z, lw, cache_k, cache_v, state_l, k_all, v_all, layer):
    o_a, state = _hgrn2(z, lw['lb_f'], lw['lb_b'], lw['gn_g'], state_l)
    o_b = _conv_module(z[:, 5 * A_W:5 * A_W + 2 * B_W], lw)
    o_ctx, k_all, v_all = _context_attention(z, k_all, v_all, layer)
    o_c = jnp.concatenate([o_ctx, _na_latent(z, cache_k, cache_v, lw['rpb'])], axis=0)
    return jnp.concatenate([o_a, o_b, o_c], axis=-1), k_all, v_all, state


def kernel(x_prompt, x_sample, cache_na_k, cache_na_v, state_hgrn, c, c_ctx, w_mod, b_mod,
           norm1_g, norm2_g, w_in, hgrn_lb, hgrn_gn_g, conv_dw_w, conv_dw_b, conv_ln_g,
           conv_ln_b, conv_pw_w, conv_pw_b, na_rpb, w_out, w_router, b_router, w_up, b_up,
           w_down, b_down, final_g):
    lb_soft = jax.nn.softmax(hgrn_lb.astype(F32), axis=1)
    lower_bounds = jnp.cumsum(lb_soft, axis=1) - lb_soft[:, :1]

    cond = jnp.zeros((8, D_MODEL), F32).at[0].set(c_ctx).at[1:1 + DEC_BATCH].set(c)
    mods = _modulation_all(cond, w_mod, b_mod)
    mods = mods[:, :N_SEQ_ROWS].reshape(DEPTH, N_SEQ_ROWS, 6, 1, D_MODEL).transpose(0, 2, 1, 3, 4)

    x = jnp.concatenate([x_prompt.reshape(N_CTX, D_MODEL), x_sample.reshape(N_LAT, D_MODEL)], axis=0)
    k_all = jnp.zeros((BATCH, DEPTH, H_C, SEQ, HEAD_DIM), F32)
    v_all = jnp.zeros((BATCH, DEPTH, H_C, SEQ, HEAD_DIM), F32)
    ss_new = []
    for l in range(DEPTH):
        sh1, sc1, g1, sh2, sc2, g2 = (mods[l, i] for i in range(6))
        lw = dict(lb_f=lower_bounds[0, l], lb_b=lower_bounds[1, l], gn_g=hgrn_gn_g[l],
                  dw_w=conv_dw_w[l], dw_b=conv_dw_b[l], ln_g=conv_ln_g[l], ln_b=conv_ln_b[l],
                  pw_w=conv_pw_w[l], pw_b=conv_pw_b[l], rpb=na_rpb[l])
        h = _norm_mod(x, norm1_g[l], sc1, sh1, BF16)
        z = _proj(h, w_in, l)
        mix, k_all, v_all, s_l = _mixers(z, lw, cache_na_k[:, l], cache_na_v[:, l], state_hgrn[:, l],
                                         k_all, v_all, l)
        ss_new.append(s_l)
        x = _proj_residual(mix, w_out, l, x, g1)

        h2, logits = _norm_mod_router(x, norm2_g[l], sc2, sh2, w_router[l], b_router[l])
        gates, dest, src_tok, blk_e, nsub, last_used = _route(logits[:, :N_EXPERTS])
        xs = _dispatch(h2, src_tok, nsub)
        y = _moe_experts(xs, blk_e, nsub, last_used, l, w_up, b_up, w_down, b_down)
        x = _combine(x, y, dest, gates, g2)

    y_all = _final_norm(x, final_g)
    y_prompt = y_all[:N_CTX].reshape(BATCH, SEQ, D_MODEL)
    y_sample = y_all[N_CTX:].reshape(DEC_BATCH, DEC_SEQ, D_MODEL)
    return (y_prompt, y_sample, k_all.transpose(0, 1, 3, 2, 4), v_all.transpose(0, 1, 3, 2, 4),
            jnp.stack(ss_new, axis=1))
```

```python
import functools

import jax
import jax.numpy as jnp
import numpy as np
from jax import lax
from jax.experimental import pallas as pl
from jax.experimental.pallas import tpu as pltpu

D_MODEL = 2048
BATCH = 32
SEQ = 256
DEPTH = 4
DEC_BATCH = 4
DEC_SEQ = 2048
PAST_LEN = 256
GRID_W = 64
HEAD_DIM = 128
A_W = 3 * D_MODEL // 8
H_A = A_W // HEAD_DIM
B_W = D_MODEL // 4
C_W = D_MODEL - A_W - B_W
H_C = C_W // HEAD_DIM
MIX_W = A_W + B_W + C_W
IN_COLS = 5 * A_W + 2 * B_W + 3 * C_W
CONV_W = 31
KH_MAX = 8
KW = 16
N_EXPERTS = 32
TOP_K = 4
D_FF = D_MODEL
SWIGLU_ALPHA = 1.702
SWIGLU_LIMIT = 7.0
EPS = 1e-6
NEG_BIG = -1e30
LB_FLOOR = 1e-30
F32 = jnp.float32
BF16 = jnp.bfloat16

N_CTX = BATCH * SEQ
N_LAT = DEC_BATCH * DEC_SEQ
N_TOK = N_CTX + N_LAT
N_SEQ_ROWS = 1 + DEC_BATCH

V7X_VMEM_LIMIT_BYTES = 56 * 1024 * 1024
V7X_VMEM_LIMIT_MOE_BYTES = 60 * 1024 * 1024

TM = 1024
TN = 1024
MOE_R = 1024
MOE_SUB = 256
MOE_TF = 512
MOE_L = N_TOK * TOP_K + N_EXPERTS * MOE_R
MOE_S = MOE_L // MOE_R
COMBINE_TOK = 256


def _seq_row(i, tm):
    n_ctx_blocks = N_CTX // tm
    per_seq = DEC_SEQ // tm
    return jnp.where(i < n_ctx_blocks, 0, 1 + (i - n_ctx_blocks) // per_seq)


def _params(sem, vmem_limit_bytes=V7X_VMEM_LIMIT_BYTES):
    return pltpu.CompilerParams(dimension_semantics=sem, vmem_limit_bytes=vmem_limit_bytes)


def _mod_kernel(c_ref, w_ref, b_ref, o_ref):
    s = c_ref[...]
    s = s * jax.nn.sigmoid(s)
    o_ref[...] = jnp.dot(s.astype(BF16), w_ref[...].astype(BF16),
                         preferred_element_type=F32) + b_ref[...]


def _modulation_all(cond, w_mod, b_mod):
    tn = 1024
    n = 6 * D_MODEL
    return pl.pallas_call(
        _mod_kernel,
        grid=(DEPTH, n // tn),
        in_specs=[
            pl.BlockSpec((8, D_MODEL), lambda l, j: (0, 0)),
            pl.BlockSpec((None, D_MODEL, tn), lambda l, j: (l, 0, j)),
            pl.BlockSpec((None, 1, tn), lambda l, j: (l, 0, j)),
        ],
        out_specs=pl.BlockSpec((None, 8, tn), lambda l, j: (l, 0, j)),
        out_shape=jax.ShapeDtypeStruct((DEPTH, 8, n), F32),
        compiler_params=_params(("arbitrary", "arbitrary")),
        name="modulation",
    )(cond, w_mod, b_mod.reshape(DEPTH, 1, n))


def _rms(x, g):
    return x * lax.rsqrt(jnp.mean(x * x, axis=-1, keepdims=True) + EPS) * g


def _norm_mod_kernel(x_ref, g_ref, sc_ref, sh_ref, o_ref):
    y = _rms(x_ref[...], g_ref[...])
    o_ref[...] = (y * (1.0 + sc_ref[...]) + sh_ref[...]).astype(o_ref.dtype)


def _norm_mod(x, g, sc, sh, dtype):
    tm = 512
    return pl.pallas_call(
        _norm_mod_kernel,
        grid=(N_TOK // tm,),
        in_specs=[
            pl.BlockSpec((tm, D_MODEL), lambda i: (i, 0)),
            pl.BlockSpec((1, D_MODEL), lambda i: (0, 0)),
            pl.BlockSpec((None, 1, D_MODEL), lambda i: (_seq_row(i, tm), 0, 0)),
            pl.BlockSpec((None, 1, D_MODEL), lambda i: (_seq_row(i, tm), 0, 0)),
        ],
        out_specs=pl.BlockSpec((tm, D_MODEL), lambda i: (i, 0)),
        out_shape=jax.ShapeDtypeStruct((N_TOK, D_MODEL), dtype),
        compiler_params=_params(("arbitrary",)),
        name="norm_mod",
    )(x, g.reshape(1, D_MODEL), sc, sh)


def _norm_mod_router_kernel(x_ref, g_ref, sc_ref, sh_ref, wr_ref, br_ref, o_ref, lg_ref):
    y = _rms(x_ref[...], g_ref[...])
    h = y * (1.0 + sc_ref[...]) + sh_ref[...]
    o_ref[...] = h.astype(o_ref.dtype)
    lg_ref[...] = jnp.dot(h, wr_ref[...], preferred_element_type=F32,
                          precision=lax.Precision.HIGHEST) + br_ref[...]


def _norm_mod_router(x, g, sc, sh, w_router, b_router):
    tm = 512
    wr = jnp.zeros((D_MODEL, 128), F32).at[:, :N_EXPERTS].set(w_router)
    br = jnp.zeros((1, 128), F32).at[0, :N_EXPERTS].set(b_router)
    return pl.pallas_call(
        _norm_mod_router_kernel,
        grid=(N_TOK // tm,),
        in_specs=[
            pl.BlockSpec((tm, D_MODEL), lambda i: (i, 0)),
            pl.BlockSpec((1, D_MODEL), lambda i: (0, 0)),
            pl.BlockSpec((None, 1, D_MODEL), lambda i: (_seq_row(i, tm), 0, 0)),
            pl.BlockSpec((None, 1, D_MODEL), lambda i: (_seq_row(i, tm), 0, 0)),
            pl.BlockSpec((D_MODEL, 128), lambda i: (0, 0)),
            pl.BlockSpec((1, 128), lambda i: (0, 0)),
        ],
        out_specs=[pl.BlockSpec((tm, D_MODEL), lambda i: (i, 0)),
                   pl.BlockSpec((tm, 128), lambda i: (i, 0))],
        out_shape=[jax.ShapeDtypeStruct((N_TOK, D_MODEL), F32),
                   jax.ShapeDtypeStruct((N_TOK, 128), F32)],
        compiler_params=_params(("arbitrary",)),
        name="norm_mod_router",
    )(x, g.reshape(1, D_MODEL), sc, sh, wr, br)


def _final_norm_kernel(x_ref, g_ref, o_ref):
    o_ref[...] = _rms(x_ref[...], g_ref[...])


def _final_norm(x, g):
    tm = 512
    return pl.pallas_call(
        _final_norm_kernel,
        grid=(N_TOK // tm,),
        in_specs=[pl.BlockSpec((tm, D_MODEL), lambda i: (i, 0)),
                  pl.BlockSpec((1, D_MODEL), lambda i: (0, 0))],
        out_specs=pl.BlockSpec((tm, D_MODEL), lambda i: (i, 0)),
        out_shape=jax.ShapeDtypeStruct((N_TOK, D_MODEL), F32),
        compiler_params=_params(("arbitrary",)),
        name="final_norm",
    )(x, g.reshape(1, D_MODEL))


def _proj_kernel(a_ref, w_ref, o_ref, wb_ref):
    @pl.when(pl.program_id(1) == 0)
    def _():
        wb_ref[...] = w_ref[...].astype(BF16)

    o_ref[...] = jnp.dot(a_ref[...], wb_ref[...], preferred_element_type=F32)


def _proj(a, w_all, layer):
    _, k, n = w_all.shape
    return pl.pallas_call(
        _proj_kernel,
        grid=(n // TN, N_TOK // TM),
        in_specs=[pl.BlockSpec((TM, k), lambda j, i: (i, 0)),
                  pl.BlockSpec((None, k, TN), lambda j, i: (layer, 0, j))],
        out_specs=pl.BlockSpec((TM, TN), lambda j, i: (i, j)),
        out_shape=jax.ShapeDtypeStruct((N_TOK, n), F32),
        scratch_shapes=[pltpu.VMEM((k, TN), BF16)],
        compiler_params=_params(("arbitrary", "arbitrary")),
        name="proj",
    )(a, w_all)


def _proj_residual_kernel(a_ref, w_ref, x_ref, gate_ref, o_ref, wb_ref):
    @pl.when(pl.program_id(1) == 0)
    def _():
        wb_ref[...] = w_ref[...].astype(BF16)

    y = jnp.dot(a_ref[...], wb_ref[...], preferred_element_type=F32)
    o_ref[...] = x_ref[...] + gate_ref[...] * y


def _proj_residual(a, w_all, layer, x, gate):
    _, k, n = w_all.shape
    return pl.pallas_call(
        _proj_residual_kernel,
        grid=(n // TN, N_TOK // TM),
        in_specs=[pl.BlockSpec((TM, k), lambda j, i: (i, 0)),
                  pl.BlockSpec((None, k, TN), lambda j, i: (layer, 0, j)),
                  pl.BlockSpec((TM, TN), lambda j, i: (i, j)),
                  pl.BlockSpec((None, 1, TN), lambda j, i: (_seq_row(i, TM), 0, j))],
        out_specs=pl.BlockSpec((TM, TN), lambda j, i: (i, j)),
        out_shape=jax.ShapeDtypeStruct((N_TOK, n), F32),
        scratch_shapes=[pltpu.VMEM((k, TN), BF16)],
        compiler_params=_params(("arbitrary", "arbitrary")),
        name="proj_residual",
    )(a, w_all, x, gate)


def _swiglu(g, u):
    g = jnp.minimum(g, SWIGLU_LIMIT)
    u = jnp.clip(u, -SWIGLU_LIMIT, SWIGLU_LIMIT)
    return g * jax.nn.sigmoid(SWIGLU_ALPHA * g) * (u + 1.0)


def _moe_kernel(blk_e_ref, nsub_ref, last_ref, xs_ref, wg_ref, wu_ref, wd_ref, bg_ref, bu_ref, bd_ref,
                o_ref, wgb_ref, wub_ref, wdb_ref):
    s = pl.program_id(0)
    f = pl.program_id(1)
    n = nsub_ref[s]

    def contribution(group, n_groups):
        rows = pl.ds(pl.multiple_of(group * MOE_SUB, MOE_SUB), n_groups * MOE_SUB)
        x = xs_ref[rows, :]
        g = jnp.dot(x, wgb_ref[...], preferred_element_type=F32) + bg_ref[...]
        u = jnp.dot(x, wub_ref[...], preferred_element_type=F32) + bu_ref[...]
        a = _swiglu(g, u).astype(BF16)
        return rows, jnp.dot(a, wdb_ref[...], preferred_element_type=F32)

    def first_chunk(group, n_groups):
        rows, y = contribution(group, n_groups)
        o_ref[rows, :] = y + bd_ref[...]

    def later_chunk(group, n_groups):
        rows, y = contribution(group, n_groups)
        o_ref[rows, :] += y

    def sweep(update):
        def pair(m, carry):
            update(2 * m, 2)
            return carry
        lax.fori_loop(0, n // 2, pair, 0)

        @pl.when(n % 2 == 1)
        def _():
            update(n - 1, 1)

    @pl.when(n > 0)
    def _():
        wgb_ref[...] = wg_ref[...].astype(BF16)
        wub_ref[...] = wu_ref[...].astype(BF16)
        wdb_ref[...] = wd_ref[...].astype(BF16)

        @pl.when(f == 0)
        def _():
            sweep(first_chunk)

        @pl.when(f > 0)
        def _():
            sweep(later_chunk)


def _moe_experts(xs, blk_e, nsub, last_used, layer, w_up, b_up, w_down, b_down):
    nf = D_FF // MOE_TF

    def blk(s, last):
        return jnp.minimum(s, last[0])

    grid_spec = pltpu.PrefetchScalarGridSpec(
        num_scalar_prefetch=3,
        grid=(MOE_S, nf),
        in_specs=[
            pl.BlockSpec((MOE_R, D_MODEL), lambda s, f, e, n, last: (blk(s, last), 0)),
            pl.BlockSpec((None, None, D_MODEL, MOE_TF),
                         lambda s, f, e, n, last: (layer, e[s], 0, jnp.where(n[s] > 0, f, nf - 1))),
            pl.BlockSpec((None, None, D_MODEL, MOE_TF),
                         lambda s, f, e, n, last: (layer, e[s], 0, nf + jnp.where(n[s] > 0, f, nf - 1))),
            pl.BlockSpec((None, None, MOE_TF, D_MODEL),
                         lambda s, f, e, n, last: (layer, e[s], jnp.where(n[s] > 0, f, nf - 1), 0)),
            pl.BlockSpec((None, None, 1, MOE_TF),
                         lambda s, f, e, n, last: (layer, e[s], 0, jnp.where(n[s] > 0, f, nf - 1))),
            pl.BlockSpec((None, None, 1, MOE_TF),
                         lambda s, f, e, n, last: (layer, e[s], 0, nf + jnp.where(n[s] > 0, f, nf - 1))),
            pl.BlockSpec((None, None, 1, D_MODEL), lambda s, f, e, n, last: (layer, e[s], 0, 0)),
        ],
        out_specs=pl.BlockSpec((MOE_R, D_MODEL), lambda s, f, e, n, last: (blk(s, last), 0)),
        scratch_shapes=[pltpu.VMEM((D_MODEL, MOE_TF), BF16),
                        pltpu.VMEM((D_MODEL, MOE_TF), BF16),
                        pltpu.VMEM((MOE_TF, D_MODEL), BF16)],
    )
    return pl.pallas_call(
        _moe_kernel,
        grid_spec=grid_spec,
        out_shape=jax.ShapeDtypeStruct((MOE_L, D_MODEL), F32),
        compiler_params=_params(("arbitrary", "arbitrary"), V7X_VMEM_LIMIT_MOE_BYTES),
        name="moe_experts",
    )(blk_e, nsub, last_used, xs, w_up, w_up, w_down,
      b_up.reshape(DEPTH, N_EXPERTS, 1, 2 * D_FF), b_up.reshape(DEPTH, N_EXPERTS, 1, 2 * D_FF),
      b_down.reshape(DEPTH, N_EXPERTS, 1, D_MODEL))


def _route(logits):
    top_val, top_idx = lax.top_k(logits, TOP_K)
    gates = jax.nn.softmax(top_val, axis=-1)
    flat_e = top_idx.reshape(-1)
    onehot = (flat_e[:, None] == jnp.arange(N_EXPERTS)[None, :]).astype(jnp.int32)
    rank = jnp.sum((jnp.cumsum(onehot, axis=0) - onehot) * onehot, axis=1)
    counts = jnp.sum(onehot, axis=0)
    padded = (counts + MOE_R - 1) // MOE_R * MOE_R
    pad_ends = jnp.cumsum(padded)
    pad_starts = pad_ends - padded
    dest = (pad_starts[flat_e] + rank).astype(jnp.int32)
    blk_start = jnp.arange(MOE_S, dtype=jnp.int32) * MOE_R
    blk_e = jnp.minimum(jnp.searchsorted(pad_ends, blk_start, side='right'),
                        N_EXPERTS - 1).astype(jnp.int32)
    used = blk_start < pad_ends[-1]
    in_blk = jnp.clip(pad_starts[blk_e] + counts[blk_e] - blk_start, 0, MOE_R)
    nsub = jnp.where(used, (in_blk + MOE_SUB - 1) // MOE_SUB, 0).astype(jnp.int32)
    last_used = (pad_ends[-1] // MOE_R - 1).astype(jnp.int32).reshape(1)
    blk_e = jnp.where(used, blk_e, blk_e[last_used[0]])
    tok = jnp.arange(N_TOK * TOP_K, dtype=jnp.int32) // TOP_K
    src_tok = jnp.zeros((MOE_L,), jnp.int32).at[dest].set(tok)
    return gates, dest, src_tok, blk_e, nsub, last_used


def _row_copy(src_hbm, src_row, dst, dst_row, sem):
    return pltpu.make_async_copy(src_hbm.at[pl.ds(src_row, 1)], dst.at[pl.ds(dst_row, 1)], sem)


ROW_ISSUE_UNROLL = 8


def _dispatch_kernel(nsub_ref, src_ref, src_next_ref, h_hbm, o_ref, buf_ref, sems):
    i = pl.program_id(0)
    n_steps = pl.num_programs(0)
    slot = i % 2

    def issue(step, s_ref, to_slot):
        def body(c, carry):
            for u in range(ROW_ISSUE_UNROLL):
                r = c * ROW_ISSUE_UNROLL + u
                _row_copy(h_hbm, s_ref[0, r], buf_ref.at[to_slot], r, sems.at[to_slot]).start()
            return carry
        lax.fori_loop(0, nsub_ref[step] * (MOE_SUB // ROW_ISSUE_UNROLL), body, 0)

    @pl.when(i == 0)
    def _():
        issue(0, src_ref, 0)

    @pl.when(i + 1 < n_steps)
    def _():
        issue(i + 1, src_next_ref, 1 - slot)

    for g in range(MOE_R // MOE_SUB):
        @pl.when(g < nsub_ref[i])
        def _():
            pltpu.make_async_copy(h_hbm.at[pl.ds(0, MOE_SUB)], buf_ref.at[slot, pl.ds(0, MOE_SUB)],
                                  sems.at[slot]).wait()

    for g in range(MOE_R // MOE_SUB):
        @pl.when(g < nsub_ref[i])
        def _():
            rows = pl.ds(g * MOE_SUB, MOE_SUB)
            o_ref[rows, :] = buf_ref[slot, rows, :].astype(o_ref.dtype)


def _dispatch(h, src_tok, nsub):
    grid_spec = pltpu.PrefetchScalarGridSpec(
        num_scalar_prefetch=1,
        grid=(MOE_S,),
        in_specs=[
            pl.BlockSpec((None, 1, MOE_R), lambda i, n: (i, 0, 0), memory_space=pltpu.SMEM),
            pl.BlockSpec((None, 1, MOE_R), lambda i, n: (jnp.minimum(i + 1, MOE_S - 1), 0, 0),
                         memory_space=pltpu.SMEM),
            pl.BlockSpec(memory_space=pl.ANY),
        ],
        out_specs=pl.BlockSpec((MOE_R, D_MODEL), lambda i, n: (i, 0)),
        scratch_shapes=[pltpu.VMEM((2, MOE_R, D_MODEL), F32),
                        pltpu.SemaphoreType.DMA((2,))],
    )
    src = src_tok.reshape(MOE_S, 1, MOE_R)
    return pl.pallas_call(
        _dispatch_kernel,
        grid_spec=grid_spec,
        out_shape=jax.ShapeDtypeStruct((MOE_L, D_MODEL), BF16),
        compiler_params=_params(("arbitrary",)),
        name="moe_dispatch",
    )(nsub, src, src, h)


def _combine_kernel(dest_ref, dest_next_ref, x_ref, y_hbm, w_ref, gate_ref, o_ref, buf_ref, sems):
    i = pl.program_id(0)
    n_steps = pl.num_programs(0)
    slot = i % 2

    def issue(d_ref, to_slot):
        for k in range(TOP_K):
            def body(c, carry):
                for u in range(ROW_ISSUE_UNROLL):
                    t = c * ROW_ISSUE_UNROLL + u
                    _row_copy(y_hbm, d_ref[0, k * COMBINE_TOK + t], buf_ref.at[to_slot, k], t,
                              sems.at[to_slot]).start()
                return carry
            lax.fori_loop(0, COMBINE_TOK // ROW_ISSUE_UNROLL, body, 0)

    @pl.when(i == 0)
    def _():
        issue(dest_ref, 0)

    @pl.when(i + 1 < n_steps)
    def _():
        issue(dest_next_ref, 1 - slot)

    for k in range(TOP_K):
        pltpu.make_async_copy(y_hbm.at[pl.ds(0, COMBINE_TOK)], buf_ref.at[slot, k], sems.at[slot]).wait()

    w = w_ref[...]
    acc = w[:, 0:1] * buf_ref[slot, 0]
    for k in range(1, TOP_K):
        acc = acc + w[:, k:k + 1] * buf_ref[slot, k]
    o_ref[...] = x_ref[...] + gate_ref[...] * acc


def _combine(x, y, dest, gates, gate2):
    tm = COMBINE_TOK
    n_steps = N_TOK // tm
    gp = jnp.zeros((N_TOK, 128), F32).at[:, :TOP_K].set(gates)
    dest_km = dest.reshape(n_steps, tm, TOP_K).transpose(0, 2, 1).reshape(n_steps, 1, TOP_K * tm)
    return pl.pallas_call(
        _combine_kernel,
        grid=(n_steps,),
        in_specs=[pl.BlockSpec((None, 1, TOP_K * tm), lambda i: (i, 0, 0), memory_space=pltpu.SMEM),
                  pl.BlockSpec((None, 1, TOP_K * tm), lambda i: (jnp.minimum(i + 1, n_steps - 1), 0, 0),
                               memory_space=pltpu.SMEM),
                  pl.BlockSpec((tm, D_MODEL), lambda i: (i, 0)),
                  pl.BlockSpec(memory_space=pl.ANY),
                  pl.BlockSpec((tm, 128), lambda i: (i, 0)),
                  pl.BlockSpec((None, 1, D_MODEL), lambda i: (_seq_row(i, tm), 0, 0))],
        out_specs=pl.BlockSpec((tm, D_MODEL), lambda i: (i, 0)),
        out_shape=jax.ShapeDtypeStruct((N_TOK, D_MODEL), F32),
        scratch_shapes=[pltpu.VMEM((2, TOP_K, tm, D_MODEL), F32),
                        pltpu.SemaphoreType.DMA((2,))],
        compiler_params=_params(("arbitrary",)),
        name="moe_combine",
    )(dest_km, dest_km, x, y, gp, gate2)


HG_T = 256
HG_LEVELS = 8
N_SEG = N_TOK // HG_T
CTX_SEGS = N_CTX // HG_T
LAT_SEGS_PER_SEQ = DEC_SEQ // HG_T
N_SEQ = BATCH + DEC_BATCH


def _hgrn_tables(reverse):
    t = np.arange(HG_T)
    r, j = t[:, None], t[None, :]
    sums = np.zeros((HG_LEVELS + 2, HG_T, HG_T), np.float32)
    pairs = np.zeros((HG_LEVELS + 1, HG_T, HG_T), np.float32)
    for l in range(HG_LEVELS):
        same_block = (r >> l) == (j >> l)
        later_half = ((r >> l) & 1) == 1
        sums[l] = same_block & np.where(later_half, j <= r, j > r)
        pairs[l] = ((r >> (l + 1)) == (j >> (l + 1))) & later_half & (((j >> l) & 1) == 0)
    sums[HG_LEVELS] = j <= r
    sums[HG_LEVELS + 1] = j > r
    pairs[HG_LEVELS] = r == j
    if reverse:
        sums = sums[:, ::-1, ::-1]
        pairs = pairs[:, ::-1, ::-1]
    return jnp.asarray(sums, BF16), jnp.asarray(pairs, F32)


def _dot_nt(x, y):
    return lax.dot_general(x, y, (((1,), (1,)), ((), ())), preferred_element_type=F32)


def _dot_tn(x, y):
    return lax.dot_general(x, y, (((0,), (0,)), ((), ())), preferred_element_type=F32)


def _hgrn_segment(g, reverse):
    seg = (N_SEG - 1 - g) if reverse else g
    lat = seg - CTX_SEGS
    first = (LAT_SEGS_PER_SEQ - 1) if reverse else 0
    starts = jnp.logical_or(seg < CTX_SEGS, lax.rem(lat, LAT_SEGS_PER_SEQ) == first)
    seq = jnp.where(seg < CTX_SEGS, seg, CTX_SEGS + lat // LAT_SEGS_PER_SEQ)
    return seg, seq, starts


HG_HEADS = 3


def _hgrn_head(qr, vr, z, la, lc, om, sums_ref, pairs_ref, state):
    q = qr * jax.nn.sigmoid(qr) * HEAD_DIM ** -0.5
    log_sig = jnp.minimum(z, 0.0) - jnp.log1p(jnp.exp(-jnp.abs(z)))
    b = lc + log_sig
    log_f = jnp.maximum(la, b) + jnp.log1p(jnp.exp(-jnp.abs(la - b)))
    k = om * jax.nn.sigmoid(-z)
    v = vr.astype(BF16)
    hi = log_f.astype(BF16)
    lo = (log_f - hi.astype(F32)).astype(BF16)
    parts = jnp.concatenate([hi, lo], axis=1)

    def block_sum(i):
        s2 = jnp.dot(sums_ref[i], parts, preferred_element_type=F32)
        return s2[:, :HEAD_DIM] + s2[:, HEAD_DIM:]

    scores = pairs_ref[HG_LEVELS] * _dot_nt(q.astype(BF16), k.astype(BF16))
    for l in range(HG_LEVELS):
        e = jnp.exp(block_sum(l))
        scores = scores + pairs_ref[l] * _dot_nt((q * e).astype(BF16), (k * e).astype(BF16))
    o = jnp.dot(scores.astype(BF16), v, preferred_element_type=F32)

    q_dec = (q * jnp.exp(block_sum(HG_LEVELS))).astype(BF16)
    o = o + jnp.dot(q_dec, state.astype(BF16), preferred_element_type=F32)

    k_dec = (k * jnp.exp(block_sum(HG_LEVELS + 1))).astype(BF16)
    total = _dot_tn(parts, jnp.ones((HG_T, HEAD_DIM), BF16))
    decay = jnp.exp(total[:HEAD_DIM] + total[HEAD_DIM:])
    return o, state * decay + _dot_tn(k_dec, v)


def _hgrn_kernel(q_ref, v_ref, z_ref, la_ref, lc_ref, om_ref, sums_ref, pairs_ref, s0_ref,
                 o_ref, sfin_ref, st_ref, *, reverse):
    _, _, starts = _hgrn_segment(pl.program_id(1), reverse)

    @pl.when(starts)
    def _():
        st_ref[...] = s0_ref[...]

    for h in range(HG_HEADS):
        cols = slice(h * HEAD_DIM, (h + 1) * HEAD_DIM)
        o, new_state = _hgrn_head(q_ref[:, cols], v_ref[:, cols], z_ref[:, cols], la_ref[:, cols],
                                  lc_ref[:, cols], om_ref[:, cols], sums_ref, pairs_ref, st_ref[h])
        o_ref[:, cols] = o
        st_ref[h] = new_state
        sfin_ref[h] = new_state


def _hgrn_scan(z, la, lc, om, s0_all, reverse):
    sums, pairs = _hgrn_tables(reverse)
    width = HG_HEADS * HEAD_DIM
    groups = H_A // HG_HEADS
    gate_col = (3 if reverse else 2) * groups

    def seg_of(g):
        return _hgrn_segment(g, reverse)[0]

    def seq_of(g):
        return _hgrn_segment(g, reverse)[1]

    row_block = lambda col0: pl.BlockSpec((HG_T, width), lambda h, g: (seg_of(g), col0 + h))
    chan = pl.BlockSpec((1, width), lambda h, g: (0, h))
    state_block = pl.BlockSpec((None, HG_HEADS, HEAD_DIM, HEAD_DIM), lambda h, g: (seq_of(g), h, 0, 0))
    return pl.pallas_call(
        functools.partial(_hgrn_kernel, reverse=reverse),
        grid=(groups, N_SEG),
        in_specs=[row_block(0), row_block(groups), row_block(gate_col), chan, chan, chan,
                  pl.BlockSpec(sums.shape, lambda h, g: (0, 0, 0)),
                  pl.BlockSpec(pairs.shape, lambda h, g: (0, 0, 0)),
                  state_block],
        out_specs=[pl.BlockSpec((HG_T, width), lambda h, g: (seg_of(g), h)), state_block],
        out_shape=[jax.ShapeDtypeStruct((N_TOK, A_W), F32),
                   jax.ShapeDtypeStruct((N_SEQ, H_A, HEAD_DIM, HEAD_DIM), F32)],
        scratch_shapes=[pltpu.VMEM((HG_HEADS, HEAD_DIM, HEAD_DIM), F32)],
        compiler_params=_params(("arbitrary", "arbitrary")),
        name="hgrn_scan_bwd" if reverse else "hgrn_scan_fwd",
    )(z, z, z, la, lc, om, sums, pairs, s0_all)


def _hgrn_finish_kernel(of_ref, ob_ref, og_ref, gn_ref, o_ref):
    o = of_ref[...] + ob_ref[...]
    og = og_ref[...]
    for h in range(H_A):
        cols = slice(h * HEAD_DIM, (h + 1) * HEAD_DIM)
        oh = o[:, cols]
        gh = og[:, cols]
        y = oh * lax.rsqrt(jnp.mean(oh * oh, axis=-1, keepdims=True) + EPS) * gn_ref[...]
        o_ref[:, cols] = (y * (gh * jax.nn.sigmoid(gh))).astype(o_ref.dtype)


def _hgrn_finish(o_f, o_b, z, gn_g):
    tm = 512
    blk = pl.BlockSpec((tm, A_W), lambda i: (i, 0))
    return pl.pallas_call(
        _hgrn_finish_kernel,
        grid=(N_TOK // tm,),
        in_specs=[blk, blk,
                  pl.BlockSpec((tm, A_W), lambda i: (i, 4)),
                  pl.BlockSpec((1, HEAD_DIM), lambda i: (0, 0))],
        out_specs=blk,
        out_shape=jax.ShapeDtypeStruct((N_TOK, A_W), BF16),
        compiler_params=_params(("arbitrary",)),
        name="hgrn_finish",
    )(o_f, o_b, z, gn_g.reshape(1, HEAD_DIM))


def _hgrn2(z, lb_f, lb_b, gn_g, state_l):
    outs, finals = [], []
    for reverse, lb in ((False, lb_f), (True, lb_b)):
        la = jnp.log(jnp.maximum(lb, LB_FLOOR)).reshape(1, A_W)
        lc = jnp.log1p(-lb).reshape(1, A_W)
        om = (1.0 - lb).reshape(1, A_W)
        s0_all = jnp.concatenate(
            [jnp.zeros((BATCH, H_A, HEAD_DIM, HEAD_DIM), F32), state_l[:, int(reverse)]], axis=0)
        o, sfin = _hgrn_scan(z, la, lc, om, s0_all, reverse)
        outs.append(o)
        finals.append(sfin[:BATCH])
    return _hgrn_finish(outs[0], outs[1], z, gn_g), jnp.stack(finals, axis=1)


CONV_HALO = 16


def _conv_kernel(prev_ref, cur_ref, next_ref, dw_ref, dwb_ref, lng_ref, lnb_ref, pw_ref, pwb_ref,
                 o_ref, ext_ref, pwb16_ref):
    i = pl.program_id(0)

    @pl.when(i == 0)
    def _():
        pwb16_ref[...] = pw_ref[...].astype(BF16)

    def glu(x):
        return x[:, :B_W] * jax.nn.sigmoid(x[:, B_W:])

    lat = i - CTX_SEGS
    pos = lax.rem(lat, LAT_SEGS_PER_SEQ)
    has_prev = jnp.logical_and(i >= CTX_SEGS, pos != 0)
    has_next = jnp.logical_and(i >= CTX_SEGS, pos != LAT_SEGS_PER_SEQ - 1)
    ext_ref[0:CONV_HALO, :] = jnp.where(has_prev, glu(prev_ref[HG_T - CONV_HALO:, :]), 0.0)
    ext_ref[CONV_HALO:CONV_HALO + HG_T, :] = glu(cur_ref[...])
    ext_ref[CONV_HALO + HG_T:, :] = jnp.where(has_next, glu(next_ref[:CONV_HALO, :]), 0.0)

    first = CONV_HALO - CONV_W // 2
    acc = ext_ref[first:first + HG_T, :] * dw_ref[0:1, :]
    for j in range(1, CONV_W):
        acc = acc + ext_ref[first + j:first + j + HG_T, :] * dw_ref[j:j + 1, :]
    u = acc + dwb_ref[...]
    mu = jnp.mean(u, axis=-1, keepdims=True)
    d = u - mu
    var = jnp.mean(d * d, axis=-1, keepdims=True)
    y = d * lax.rsqrt(var + EPS) * lng_ref[...] + lnb_ref[...]
    y = y * jax.nn.sigmoid(y)
    o = jnp.dot(y.astype(BF16), pwb16_ref[...], preferred_element_type=F32) + pwb_ref[...]
    o_ref[...] = o.astype(o_ref.dtype)


def _conv_module(glu_in, lw):
    row = lambda d: pl.BlockSpec((1, B_W), lambda i: (0, 0))
    seg = lambda off: pl.BlockSpec((HG_T, 2 * B_W), lambda i: (jnp.clip(i + off, 0, N_SEG - 1), 0))
    vec = lambda a: a.reshape(1, B_W)
    return pl.pallas_call(
        _conv_kernel,
        grid=(N_SEG,),
        in_specs=[seg(-1), seg(0), seg(1),
                  pl.BlockSpec((CONV_W, B_W), lambda i: (0, 0)), row(0), row(0), row(0),
                  pl.BlockSpec((B_W, B_W), lambda i: (0, 0)), row(0)],
        out_specs=pl.BlockSpec((HG_T, B_W), lambda i: (i, 0)),
        out_shape=jax.ShapeDtypeStruct((N_TOK, B_W), BF16),
        scratch_shapes=[pltpu.VMEM((HG_T + 2 * CONV_HALO, B_W), F32),
                        pltpu.VMEM((B_W, B_W), BF16)],
        compiler_params=_params(("arbitrary",)),
        name="conv_module",
    )(glu_in, glu_in, glu_in, lw['dw_w'], vec(lw['dw_b']), vec(lw['ln_g']), vec(lw['ln_b']),
      lw['pw_w'], vec(lw['pw_b']))


COL_GLU = 5 * H_A
COL_QC = COL_GLU + 2 * B_W // HEAD_DIM
COL_KC = COL_QC + H_C
COL_VC = COL_KC + H_C

NA_QROWS = 4
NA_QB = NA_QROWS * GRID_W
NA_KROWS = NA_QROWS + KH_MAX
NA_KEYS = NA_KROWS * GRID_W
NA_BLOCKS = DEC_SEQ // NA_QB
GRID_ROWS = DEC_SEQ // GRID_W


def _na_key_row0(first_query_row):
    clip = np.clip if isinstance(first_query_row, (int, np.integer)) else jnp.clip
    return clip(first_query_row - KH_MAX // 2, 0, GRID_ROWS - NA_KROWS)


def _na_bias_tables(rpb):
    kh = min(KH_MAX, GRID_ROWS)
    n_dr, n_dc = 2 * KH_MAX - 1, 2 * KW - 1
    c = np.arange(GRID_W)[:, None]
    kc = np.arange(GRID_W)[None, :]
    col_start = np.clip(c - KW // 2, 0, GRID_W - KW)
    col_valid = (kc >= col_start) & (kc < col_start + KW)
    dc = np.clip(kc - c, -(KW - 1), KW - 1) + KW - 1
    pick_dc = (dc.reshape(-1)[None, :] == np.arange(n_dc)[:, None]).astype(np.float32)
    pick_dr = np.zeros((4, NA_QROWS * NA_KROWS, n_dr), np.float32)
    row_valid = np.zeros((4, NA_QROWS, NA_KROWS), bool)
    for t, first_row in enumerate((0, NA_QROWS, 2 * NA_QROWS, GRID_ROWS - NA_QROWS)):
        r = first_row + np.arange(NA_QROWS)[:, None]
        kr = int(_na_key_row0(first_row)) + np.arange(NA_KROWS)[None, :]
        row_start = np.clip(r - kh // 2, 0, GRID_ROWS - kh)
        row_valid[t] = (kr >= row_start) & (kr < row_start + kh)
        dr = kr - r + KH_MAX - 1
        pick_dr[t] = (dr.reshape(-1)[:, None] == np.arange(n_dr)[None, :]) & row_valid[t].reshape(-1)[:, None]
    by_col = jnp.einsum('hrd,dx->hrx', rpb.astype(F32), pick_dc, precision=lax.Precision.HIGHEST)
    table = jnp.einsum('tpr,hrx->thpx', pick_dr, by_col, precision=lax.Precision.HIGHEST)
    table = table.reshape(4, H_C, NA_QROWS, NA_KROWS, GRID_W, GRID_W).transpose(0, 1, 2, 4, 3, 5)
    valid = row_valid[:, None, :, None, :, None] & col_valid[None, None, None, :, None, :]
    return jnp.where(valid, table, NEG_BIG).reshape(4, H_C, NA_QB, NA_KEYS)


def _softmax_pv(scores, values):
    m = functools.reduce(jnp.maximum, [jnp.max(s, axis=-1, keepdims=True) for s in scores])
    ps = [jnp.exp(s - m) for s in scores]
    denom = functools.reduce(jnp.add, [jnp.sum(p, axis=-1, keepdims=True) for p in ps])
    o = functools.reduce(jnp.add, [jnp.dot(p.astype(BF16), v, preferred_element_type=F32)
                                   for p, v in zip(ps, values)])
    return o / denom


def _na_kernel(q_ref, k_ref, v_ref, kc_ref, vc_ref, bias_ref, o_ref):
    key_row0 = _na_key_row0(pl.program_id(2) * NA_QROWS)
    keys = pl.ds(pl.multiple_of(key_row0 * GRID_W, NA_QB), NA_KEYS)
    scale = HEAD_DIM ** -0.5
    q = q_ref[...].astype(BF16)
    s_loc = _dot_nt(q, k_ref[keys, :].astype(BF16)) * scale + bias_ref[...]
    s_ctx = _dot_nt(q, kc_ref[...].astype(BF16)) * scale
    o = _softmax_pv([s_loc, s_ctx], [v_ref[keys, :].astype(BF16), vc_ref[...].astype(BF16)])
    o_ref[...] = o.astype(o_ref.dtype)


def _na_latent(z, cache_k, cache_v, rpb):
    bias = _na_bias_tables(rpb)
    q_blocks0 = N_CTX // NA_QB
    seq_blocks0 = N_CTX // DEC_SEQ

    def block_type(qb):
        return jnp.where(qb < 2, qb, jnp.where(qb == NA_BLOCKS - 1, 3, 2))

    seq_spec = lambda col0: pl.BlockSpec((DEC_SEQ, HEAD_DIM), lambda b, h, qb: (seq_blocks0 + b, col0 + h))
    ctx_spec = pl.BlockSpec((None, PAST_LEN, HEAD_DIM), lambda b, h, qb: (b, 0, h))
    return pl.pallas_call(
        _na_kernel,
        grid=(DEC_BATCH, H_C, NA_BLOCKS),
        in_specs=[pl.BlockSpec((NA_QB, HEAD_DIM), lambda b, h, qb: (q_blocks0 + b * NA_BLOCKS + qb, COL_QC + h)),
                  seq_spec(COL_KC), seq_spec(COL_VC), ctx_spec, ctx_spec,
                  pl.BlockSpec((None, None, NA_QB, NA_KEYS), lambda b, h, qb: (block_type(qb), h, 0, 0))],
        out_specs=pl.BlockSpec((NA_QB, HEAD_DIM), lambda b, h, qb: (b * NA_BLOCKS + qb, h)),
        out_shape=jax.ShapeDtypeStruct((N_LAT, C_W), BF16),
        compiler_params=_params(("arbitrary", "arbitrary", "arbitrary")),
        name="na_attention",
    )(z, z, z, cache_k.reshape(DEC_BATCH, PAST_LEN, C_W), cache_v.reshape(DEC_BATCH, PAST_LEN, C_W), bias)


def _ctx_attn_kernel(q_ref, k_ref, v_ref, k_all_ref, v_all_ref, o_ref, k_out_ref, v_out_ref):
    del k_all_ref, v_all_ref
    scale = HEAD_DIM ** -0.5
    k = k_ref[...]
    v = v_ref[...]
    k_out_ref[...] = k
    v_out_ref[...] = v
    s = _dot_nt(q_ref[...].astype(BF16), k.astype(BF16)) * scale
    o_ref[...] = _softmax_pv([s], [v.astype(BF16)]).astype(o_ref.dtype)


def _context_attention(z, k_all, v_all, layer):
    spec = lambda col0: pl.BlockSpec((SEQ, HEAD_DIM), lambda b, h: (b, col0 + h))
    per_head = pl.BlockSpec((None, None, None, SEQ, HEAD_DIM), lambda b, h: (b, layer, h, 0, 0))
    kv_shape = jax.ShapeDtypeStruct((BATCH, DEPTH, H_C, SEQ, HEAD_DIM), F32)
    whole = pl.BlockSpec(memory_space=pl.ANY)
    return pl.pallas_call(
        _ctx_attn_kernel,
        grid=(BATCH, H_C),
        in_specs=[spec(COL_QC), spec(COL_KC), spec(COL_VC), whole, whole],
        out_specs=[pl.BlockSpec((SEQ, HEAD_DIM), lambda b, h: (b, h)), per_head, per_head],
        out_shape=[jax.ShapeDtypeStruct((N_CTX, C_W), BF16), kv_shape, kv_shape],
        input_output_aliases={3: 1, 4: 2},
        compiler_params=_params(("arbitrary", "arbitrary")),
        name="context_attention",
    )(z, z, z, k_all, v_all)


def _mixers(z, lw, cache_k, cache_v, state_l, k_all, v_all, layer):
    o_a, state = _hgrn2(z, lw['lb_f'], lw['lb_b'], lw['gn_g'], state_l)
    o_b = _conv_module(z[:, 5 * A_W:5 * A_W + 2 * B_W], lw)
    o_ctx, k_all, v_all = _context_attention(z, k_all, v_all, layer)
    o_c = jnp.concatenate([o_ctx, _na_latent(z, cache_k, cache_v, lw['rpb'])], axis=0)
    return jnp.concatenate([o_a, o_b, o_c], axis=-1), k_all, v_all, state


def kernel(x_prompt, x_sample, cache_na_k, cache_na_v, state_hgrn, c, c_ctx, w_mod, b_mod,
           norm1_g, norm2_g, w_in, hgrn_lb, hgrn_gn_g, conv_dw_w, conv_dw_b, conv_ln_g,
           conv_ln_b, conv_pw_w, conv_pw_b, na_rpb, w_out, w_router, b_router, w_up, b_up,
           w_down, b_down, final_g):
    lb_soft = jax.nn.softmax(hgrn_lb.astype(F32), axis=1)
    lower_bounds = jnp.cumsum(lb_soft, axis=1) - lb_soft[:, :1]

    cond = jnp.zeros((8, D_MODEL), F32).at[0].set(c_ctx).at[1:1 + DEC_BATCH].set(c)
    mods = _modulation_all(cond, w_mod, b_mod)
    mods = mods[:, :N_SEQ_ROWS].reshape(DEPTH, N_SEQ_ROWS, 6, 1, D_MODEL).transpose(0, 2, 1, 3, 4)

    x = jnp.concatenate([x_prompt.reshape(N_CTX, D_MODEL), x_sample.reshape(N_LAT, D_MODEL)], axis=0)
    k_all = jnp.zeros((BATCH, DEPTH, H_C, SEQ, HEAD_DIM), F32)
    v_all = jnp.zeros((BATCH, DEPTH, H_C, SEQ, HEAD_DIM), F32)
    ss_new = []
    for l in range(DEPTH):
        sh1, sc1, g1, sh2, sc2, g2 = (mods[l, i] for i in range(6))
        lw = dict(lb_f=lower_bounds[0, l], lb_b=lower_bounds[1, l], gn_g=hgrn_gn_g[l],
                  dw_w=conv_dw_w[l], dw_b=conv_dw_b[l], ln_g=conv_ln_g[l], ln_b=conv_ln_b[l],
                  pw_w=conv_pw_w[l], pw_b=conv_pw_b[l], rpb=na_rpb[l])
        h = _norm_mod(x, norm1_g[l], sc1, sh1, BF16)
        z = _proj(h, w_in, l)
        mix, k_all, v_all, s_l = _mixers(z, lw, cache_na_k[:, l], cache_na_v[:, l], state_hgrn[:, l],
                                         k_all, v_all, l)
        ss_new.append(s_l)
        x = _proj_residual(mix, w_out, l, x, g1)

        h2, logits = _norm_mod_router(x, norm2_g[l], sc2, sh2, w_router[l], b_router[l])
        gates, dest, src_tok, blk_e, nsub, last_used = _route(logits[:, :N_EXPERTS])
        xs = _dispatch(h2, src_tok, nsub)
        y = _moe_experts(xs, blk_e, nsub, last_used, l, w_up, b_up, w_down, b_down)
        x = _combine(x, y, dest, gates, g2)

    y_all = _final_norm(x, final_g)
    y_prompt = y_all[:N_CTX].reshape(BATCH, SEQ, D_MODEL)
    y_sample = y_all[N_CTX:].reshape(DEC_BATCH, DEC_SEQ, D_MODEL)
    return (y_prompt, y_sample, k_all.transpose(0, 1, 3, 2, 4), v_all.transpose(0, 1, 3, 2, 4),
            jnp.stack(ss_new, axis=1))
```
